```python
import math
import jax, jax.numpy as jnp
from jax import lax
import numpy as np

D_MODEL = 1024
BATCH = 2
SEQ = 8192
DEPTH = 4
DEC_BATCH = 32
DEC_SEQ = 4
PAST_LEN = 8192
PAGE_SIZE = 128

N_MIXERS = 3
N_NSA = (DEPTH + 2) // 3
N_GDN = (DEPTH + 1) // 3
N_S5 = DEPTH // 3
N_DENSE = (DEPTH + 1) // 2
N_MOE = DEPTH // 2

NSA_HEADS = 16
HEAD_DIM = D_MODEL // NSA_HEADS
NSA_KV_HEADS = 2
NSA_HPG = NSA_HEADS // NSA_KV_HEADS
CMP_BLOCK = 32
CMP_STRIDE = 16
CMP_HIDDEN = 128
SEL_BLOCK = 64
SEL_TOPK = 16
WINDOW = 512
NSA_Q_BLOCK = 128
NSA_Q_DIM = NSA_HEADS * HEAD_DIM
NSA_KV_DIM = 6 * NSA_KV_HEADS * HEAD_DIM
NSA_IN = NSA_Q_DIM + NSA_KV_DIM + 3 * NSA_HEADS

GDN_HEADS = 8
GDN_DK = 128
GDN_DV = 128
GDN_DIM = GDN_HEADS * GDN_DK
GDN_IN = 4 * GDN_DIM + 2 * GDN_HEADS
CONV_WIDTH = 4
GDN_CHUNK = 64

S5_GROUP = 16
S5_GROUPS = D_MODEL // S5_GROUP
S5_STATE = 64

D_FF = 2816
N_EXPERTS = 8
TOP_K = 2
D_FF_EXPERT = 3584
MOE_BLOCK = 256

RMS_EPS = 1e-6
NEG_INF = -1e30

kernel_name = 'hybrid_nsa_gdn_s5_decode_step'


def rmsnorm(x, w):
    xf = x.astype(jnp.float32)
    y = xf * lax.rsqrt(jnp.mean(xf * xf, axis=-1, keepdims=True) + RMS_EPS)
    return (y * w.astype(jnp.float32)).astype(x.dtype)


def adaln_mods(c, w, b):
    m = (jax.nn.silu(c) @ w + b)[:, None, :]
    return jnp.split(m, 6, axis=-1)


def alibi_slopes():
    return jnp.exp2(-8.0 * (jnp.arange(NSA_HEADS, dtype=jnp.float32) + 1.0) / NSA_HEADS)


def nsa_project(h, w_in):
    B, T, _ = h.shape
    proj = h @ w_in
    q = proj[..., :NSA_Q_DIM].reshape(B, T, NSA_HEADS, HEAD_DIM)
    kv = proj[..., NSA_Q_DIM:NSA_Q_DIM + NSA_KV_DIM].reshape(B, T, 6, NSA_KV_HEADS, HEAD_DIM)
    gates = jax.nn.sigmoid(proj[..., NSA_Q_DIM + NSA_KV_DIM:].astype(jnp.float32)).reshape(B, T, NSA_HEADS, 3)
    return q, kv, gates


def nsa_compress(seq, pe, w1, w2):
    B, L, G, dh = seq.shape
    r = CMP_BLOCK // CMP_STRIDE
    n_chunk = L // CMP_STRIDE
    nc = n_chunk - r + 1
    ch = seq.reshape(B, n_chunk, CMP_STRIDE, G, dh)
    blocks = jnp.concatenate([ch[:, j:j + nc] for j in range(r)], axis=2)
    blocks = blocks + pe[None, None, :, None, :]
    flat = blocks.transpose(0, 1, 3, 2, 4).reshape(B, nc, G, CMP_BLOCK * dh)
    return jax.nn.gelu(flat @ w1) @ w2


def cmp_to_sel_map(nc, nsb):
    c_start = jnp.arange(nc) * CMP_STRIDE
    s_start = jnp.arange(nsb) * SEL_BLOCK
    hit = (c_start[:, None] < s_start[None, :] + SEL_BLOCK) & (c_start[:, None] + CMP_BLOCK > s_start[None, :])
    return hit.astype(jnp.float32)


def nsa_attend_block(q, gates, q_pos, kc, vc, kc_pos, ks_t, vs_t, kw, vw, kw_pos):
    B, Tq = q.shape[:2]
    nc = kc.shape[1]
    nsb = ks_t.shape[2]
    n_sel = min(SEL_TOPK, nsb)
    slopes = alibi_slopes().reshape(NSA_KV_HEADS, NSA_HPG)
    qg = q.reshape(B, Tq, NSA_KV_HEADS, NSA_HPG, HEAD_DIM) * (HEAD_DIM ** -0.5)

    d_c = q_pos[:, None] - kc_pos[None, :]
    ok_c = (d_c >= 0)[None, :, None, None, :]
    s_c = jnp.einsum('btghd,bngd->btghn', qg, kc).astype(jnp.float32)
    s_c = s_c - slopes[None, None, :, :, None] * jnp.abs(d_c).astype(jnp.float32)[None, :, None, None, :]
    s_c = jnp.where(ok_c, s_c, NEG_INF)
    p_c = jax.nn.softmax(s_c, axis=-1) * ok_c
    o_c = jnp.einsum('btghn,bngd->btghd', p_c.astype(vc.dtype), vc)

    imp = jnp.einsum('btghn,nj->btgj', p_c, cmp_to_sel_map(nc, nsb))
    cur = q_pos // SEL_BLOCK
    blk = jnp.arange(nsb)
    is_cur = (blk[None, :] == cur[:, None])[None, :, None, :]
    is_past = (blk[None, :] < cur[:, None])[None, :, None, :]
    imp = jnp.where(is_cur, jnp.inf, jnp.where(is_past, imp, -jnp.inf))
    _, idx = lax.top_k(imp, n_sel)
    bi = jnp.arange(B)[:, None, None, None]
    gi = jnp.arange(NSA_KV_HEADS)[None, None, :, None]
    k_sel = ks_t[bi, gi, idx]
    v_sel = vs_t[bi, gi, idx]
    pos_s = idx[..., None] * SEL_BLOCK + jnp.arange(SEL_BLOCK)
    d_s = q_pos[None, :, None, None, None] - pos_s
    ok_s = (d_s >= 0)[:, :, :, None]
    s_s = jnp.einsum('btghd,btgnkd->btghnk', qg, k_sel).astype(jnp.float32)
    s_s = s_s - slopes[None, None, :, :, None, None] * jnp.abs(d_s).astype(jnp.float32)[:, :, :, None]
    s_s = jnp.where(ok_s, s_s, NEG_INF).reshape(B, Tq, NSA_KV_HEADS, NSA_HPG, n_sel * SEL_BLOCK)
    p_s = jax.nn.softmax(s_s, axis=-1).reshape(B, Tq, NSA_KV_HEADS, NSA_HPG, n_sel, SEL_BLOCK)
    o_s = jnp.einsum('btghnk,btgnkd->btghd', p_s.astype(v_sel.dtype), v_sel)

    d_w = q_pos[:, None] - kw_pos[None, :]
    ok_w = ((d_w >= 0) & (d_w < WINDOW) & (kw_pos >= 0)[None, :])[None, :, None, None, :]
    s_w = jnp.einsum('btghd,blgd->btghl', qg, kw).astype(jnp.float32)
    s_w = s_w - slopes[None, None, :, :, None] * jnp.abs(d_w).astype(jnp.float32)[None, :, None, None, :]
    p_w = jax.nn.softmax(jnp.where(ok_w, s_w, NEG_INF), axis=-1)
    o_w = jnp.einsum('btghl,blgd->btghd', p_w.astype(vw.dtype), vw)

    g = gates.reshape(B, Tq, NSA_KV_HEADS, NSA_HPG, 3).astype(q.dtype)
    o = g[..., 0:1] * o_c + g[..., 1:2] * o_s + g[..., 2:3] * o_w
    return o.reshape(B, Tq, NSA_Q_DIM)


def nsa_prompt(h, w_in, pe, w1, w2, w_out):
    B, T, _ = h.shape
    q, kv, gates = nsa_project(h, w_in)
    kc = nsa_compress(kv[:, :, 0], pe[0], w1[0], w2[0])
    vc = nsa_compress(kv[:, :, 1], pe[1], w1[1], w2[1])
    kc_pos = jnp.arange(kc.shape[1]) * CMP_STRIDE + CMP_BLOCK - 1
    nsb = T // SEL_BLOCK
    ks_t = kv[:, :, 2].reshape(B, nsb, SEL_BLOCK, NSA_KV_HEADS, HEAD_DIM).transpose(0, 3, 1, 2, 4)
    vs_t = kv[:, :, 3].reshape(B, nsb, SEL_BLOCK, NSA_KV_HEADS, HEAD_DIM).transpose(0, 3, 1, 2, 4)
    win_pad = jnp.pad(kv[:, :, 4:], ((0, 0), (WINDOW, 0), (0, 0), (0, 0), (0, 0)))

    def q_block(i):
        start = i * NSA_Q_BLOCK
        qb = lax.dynamic_slice_in_dim(q, start, NSA_Q_BLOCK, axis=1)
        gb = lax.dynamic_slice_in_dim(gates, start, NSA_Q_BLOCK, axis=1)
        wb = lax.dynamic_slice_in_dim(win_pad, start, WINDOW + NSA_Q_BLOCK, axis=1)
        q_pos = start + jnp.arange(NSA_Q_BLOCK)
        kw_pos = start - WINDOW + jnp.arange(WINDOW + NSA_Q_BLOCK)
        return nsa_attend_block(qb, gb, q_pos, kc, vc, kc_pos, ks_t, vs_t, wb[:, :, 0], wb[:, :, 1], kw_pos)

    o = lax.map(q_block, jnp.arange(T // NSA_Q_BLOCK))
    o = o.transpose(1, 0, 2, 3).reshape(B, T, NSA_Q_DIM)
    return o @ w_out, kv[:, :, :4], kv[:, T - min(WINDOW, T):, 4:]


def nsa_sample(h, cache_kv, cache_win, page_table, w_in, pe, w1, w2, w_out):
    B, T, _ = h.shape
    past_len = page_table.shape[1] * cache_kv.shape[1]
    win_len = cache_win.shape[1]
    q, kv, gates = nsa_project(h, w_in)
    past = cache_kv[page_table].reshape(B, past_len, 4, NSA_KV_HEADS, HEAD_DIM)
    full = jnp.concatenate([past.astype(kv.dtype), kv[:, :, :4]], axis=1)
    L = past_len + T
    Lp = -(-L // SEL_BLOCK) * SEL_BLOCK
    full = jnp.pad(full, ((0, 0), (0, Lp - L), (0, 0), (0, 0), (0, 0)))
    kc = nsa_compress(full[:, :, 0], pe[0], w1[0], w2[0])
    vc = nsa_compress(full[:, :, 1], pe[1], w1[1], w2[1])
    kc_pos = jnp.arange(kc.shape[1]) * CMP_STRIDE + CMP_BLOCK - 1
    nsb = Lp // SEL_BLOCK
    ks_t = full[:, :, 2].reshape(B, nsb, SEL_BLOCK, NSA_KV_HEADS, HEAD_DIM).transpose(0, 3, 1, 2, 4)
    vs_t = full[:, :, 3].reshape(B, nsb, SEL_BLOCK, NSA_KV_HEADS, HEAD_DIM).transpose(0, 3, 1, 2, 4)
    win = jnp.concatenate([cache_win.astype(kv.dtype), kv[:, :, 4:]], axis=1)
    kw_pos = past_len - win_len + jnp.arange(win_len + T)
    q_pos = past_len + jnp.arange(T)
    o = nsa_attend_block(q, gates, q_pos, kc, vc, kc_pos, ks_t, vs_t, win[:, :, 0], win[:, :, 1], kw_pos)
    return o @ w_out, kv[:, :, :4], win[:, win.shape[1] - win_len:]


def causal_dwconv(xp, w):
    return lax.conv_general_dilated(xp, w[:, None, :], window_strides=(1,), padding='VALID',
                                    dimension_numbers=('NWC', 'WIO', 'NWC'),
                                    feature_group_count=xp.shape[-1])


def l2norm(x):
    return x * lax.rsqrt(jnp.sum(x * x, axis=-1, keepdims=True) + 1e-6)


def gated_delta_chunked(q, k, v, g, beta, s0, chunk):
    B, T, H, DK = q.shape
    DV = v.shape[-1]
    n = T // chunk

    def blk(a):
        return jnp.moveaxis(a.reshape(B, n, chunk, H, *a.shape[3:]), 3, 2)

    q, k, v, g, beta = blk(q), blk(k), blk(v), blk(g), blk(beta)
    gc = jnp.cumsum(g, axis=-1)
    causal = jnp.tril(jnp.ones((chunk, chunk), dtype=bool))
    strict = jnp.tril(jnp.ones((chunk, chunk), dtype=bool), -1)
    diff = gc[..., :, None] - gc[..., None, :]
    decay = jnp.where(causal, jnp.exp(jnp.where(causal, diff, 0.0)), 0.0)
    kk = jnp.einsum('bnhik,bnhjk->bnhij', k, k)
    lower = jnp.where(strict, beta[..., :, None] * kk * decay, 0.0)
    a_mat = lower + jnp.eye(chunk, dtype=lower.dtype)
    rhs = jnp.concatenate([v * beta[..., None], k * (beta * jnp.exp(gc))[..., None]], axis=-1)
    sol = lax.linalg.triangular_solve(a_mat, rhs, left_side=True, lower=True, unit_diagonal=True)
    u, w = sol[..., :DV], sol[..., DV:]
    qk = jnp.einsum('bnhik,bnhjk->bnhij', q, k) * decay
    q_dec = q * jnp.exp(gc)[..., None]
    k_dec = k * jnp.exp(gc[..., -1:] - gc)[..., None]
    g_last = jnp.exp(gc[..., -1])

    def step(s, xs):
        u_c, w_c, qk_c, qd_c, kd_c, gl_c = xs
        v_new = u_c - jnp.einsum('bhck,bhkv->bhcv', w_c, s)
        o_c = jnp.einsum('bhck,bhkv->bhcv', qd_c, s) + jnp.einsum('bhij,bhjv->bhiv', qk_c, v_new)
        s = s * gl_c[..., None, None] + jnp.einsum('bhck,bhcv->bhkv', kd_c, v_new)
        return s, o_c

    xs = tuple(jnp.moveaxis(a, 1, 0) for a in (u, w, qk, q_dec, k_dec, g_last))
    s, o = lax.scan(step, s0, xs)
    o = jnp.moveaxis(jnp.moveaxis(o, 0, 1), 2, 3).reshape(B, T, H, DV)
    return o, s


def gdn_mixer(h, conv_buf, s0, w_in, conv_w, a_log, dt_bias, norm_w, w_out, chunk):
    B, T, _ = h.shape
    f32 = jnp.float32
    proj = h @ w_in
    qkv_raw = proj[..., :3 * GDN_DIM]
    z = proj[..., 3 * GDN_DIM:4 * GDN_DIM].reshape(B, T, GDN_HEADS, GDN_DV)
    a = proj[..., 4 * GDN_DIM:4 * GDN_DIM + GDN_HEADS].astype(f32)
    b = proj[..., 4 * GDN_DIM + GDN_HEADS:].astype(f32)
    xpad = jnp.concatenate([conv_buf.astype(qkv_raw.dtype), qkv_raw], axis=1)
    qkv = jax.nn.silu(causal_dwconv(xpad, conv_w)).astype(f32)
    q, k, v = jnp.split(qkv, 3, axis=-1)
    q = l2norm(q.reshape(B, T, GDN_HEADS, GDN_DK)) * (GDN_DK ** -0.5)
    k = l2norm(k.reshape(B, T, GDN_HEADS, GDN_DK))
    v = v.reshape(B, T, GDN_HEADS, GDN_DV)
    beta = jax.nn.sigmoid(b)
    g = -jnp.exp(a_log.astype(f32)) * jax.nn.softplus(a + dt_bias.astype(f32))
    o, s = gated_delta_chunked(q, k, v, g, beta, s0.astype(f32), chunk)
    o = rmsnorm(o, norm_w) * jax.nn.silu(z.astype(f32))
    out = o.reshape(B, T, GDN_HEADS * GDN_DV).astype(h.dtype) @ w_out
    return out, xpad[:, xpad.shape[1] - (CONV_WIDTH - 1):], s


def complex_linear_combine(e1, e2):
    a1r, a1i, b1r, b1i = e1
    a2r, a2i, b2r, b2i = e2
    return (a2r * a1r - a2i * a1i, a2r * a1i + a2i * a1r,
            a2r * b1r - a2i * b1i + b2r, a2r * b1i + a2i * b1r + b2i)


def s5_mixer(h, h0_re, h0_im, w_in, lam_re, lam_im, b_re, b_im, c_re, c_im, d_skip, log_dt, w_glu, w_out):
    B, T, _ = h.shape
    f32 = jnp.float32
    u = (h @ w_in).astype(f32)
    ug = u.reshape(B, T, S5_GROUPS, S5_GROUP)
    dt = jnp.exp(log_dt.astype(f32))[:, None]
    lr, li = lam_re.astype(f32), lam_im.astype(f32)
    mag = jnp.exp(lr * dt)
    ar, ai = mag * jnp.cos(li * dt), mag * jnp.sin(li * dt)
    den = lr * lr + li * li
    fr = ((ar - 1.0) * lr + ai * li) / den
    fi = (ai * lr - (ar - 1.0) * li) / den
    br, bim = b_re.astype(f32), b_im.astype(f32)
    bbar_re = fr[..., None] * br - fi[..., None] * bim
    bbar_im = fr[..., None] * bim + fi[..., None] * br
    bu_re = jnp.einsum('gpc,btgc->btgp', bbar_re, ug)
    bu_im = jnp.einsum('gpc,btgc->btgp', bbar_im, ug)
    h0r, h0i = h0_re.astype(f32), h0_im.astype(f32)
    bu_re = bu_re.at[:, 0].add(ar * h0r - ai * h0i)
    bu_im = bu_im.at[:, 0].add(ar * h0i + ai * h0r)
    a_re = jnp.broadcast_to(ar, bu_re.shape)
    a_im = jnp.broadcast_to(ai, bu_im.shape)
    _, _, hr, hi = lax.associative_scan(complex_linear_combine, (a_re, a_im, bu_re, bu_im), axis=1)
    y = jnp.einsum('gcp,btgp->btgc', c_re.astype(f32), hr) - jnp.einsum('gcp,btgp->btgc', c_im.astype(f32), hi)
    y = y.reshape(B, T, D_MODEL) + d_skip.astype(f32) * u
    z = jax.nn.gelu(y)
    z = z * jax.nn.sigmoid(z @ w_glu.astype(f32))
    return z.astype(h.dtype) @ w_out, hr[:, -1], hi[:, -1]


def swiglu(h, w_gu, w_down):
    a, b = jnp.split(h @ w_gu, 2, axis=-1)
    return (jax.nn.silu(a) * b) @ w_down


def moe_swiglu(h, w_router, w_gu, w_down):
    n_tok, d = h.shape
    n_rows = n_tok * TOP_K
    logits = (h @ w_router).astype(jnp.float32)
    top_logit, top_e = lax.top_k(logits, TOP_K)
    gate = jax.nn.softmax(top_logit, axis=-1).reshape(-1)
    flat_e = top_e.reshape(-1)
    order = jnp.argsort(flat_e)
    e_sorted = flat_e[order]
    tok_sorted = order // TOP_K
    gate_sorted = gate[order]
    counts = jnp.bincount(flat_e, length=N_EXPERTS)
    padded = (counts + MOE_BLOCK - 1) // MOE_BLOCK * MOE_BLOCK
    pad_end = jnp.cumsum(padded)
    pad_start = pad_end - padded
    grp_start = jnp.cumsum(counts) - counts
    dest = pad_start[e_sorted] + jnp.arange(n_rows) - grp_start[e_sorted]
    n_blocks = -(-n_rows // MOE_BLOCK) + N_EXPERTS
    rows = jnp.zeros((n_blocks * MOE_BLOCK, d), h.dtype).at[dest].set(h[tok_sorted])
    block_e = jnp.minimum(jnp.searchsorted(pad_end, jnp.arange(n_blocks) * MOE_BLOCK, side='right'), N_EXPERTS - 1)

    def expert_block(args):
        xb, e = args
        a, b = jnp.split(xb @ w_gu[e], 2, axis=-1)
        return (jax.nn.silu(a) * b) @ w_down[e]

    out = lax.map(expert_block, (rows.reshape(n_blocks, MOE_BLOCK, d), block_e)).reshape(-1, d)
    contrib = out[dest] * gate_sorted[:, None].astype(h.dtype)
    return jnp.zeros_like(h).at[tok_sorted].add(contrib)


def setup_inputs(seed: int = 0) -> dict:
    key = jax.random.key(seed)
    keys = jax.random.split(key, 64)
    cnt = [0]

    def nk():
        cnt[0] += 1
        return keys[cnt[0] - 1]

    def nrm(shape, scale=1.0):
        return jax.random.normal(nk(), shape, jnp.float32) * scale

    def unif(shape, lo, hi):
        return jax.random.uniform(nk(), shape, jnp.float32, lo, hi)

    d = D_MODEL
    n_pages = PAST_LEN // PAGE_SIZE
    n_used = DEC_BATCH * n_pages
    n_pool = n_used + max(1, n_used // 4)
    win_len = min(WINDOW, PAST_LEN)
    page_table = jax.random.permutation(nk(), n_pool)[:n_used].reshape(DEC_BATCH, n_pages).astype(jnp.int32)
    dt_g = jnp.exp(unif((N_GDN, GDN_HEADS), math.log(1e-3), math.log(1e-1)))
    s5_n = jnp.arange(S5_STATE, dtype=jnp.float32)
    return {
        'x_prompt': nrm((BATCH, SEQ, d)),
        'x_sample': nrm((DEC_BATCH, DEC_SEQ, d)),
        'cache_nsa_kv': nrm((N_NSA, n_pool, PAGE_SIZE, 4, NSA_KV_HEADS, HEAD_DIM)),
        'cache_nsa_win': nrm((N_NSA, DEC_BATCH, win_len, 2, NSA_KV_HEADS, HEAD_DIM)),
        'state_gdn_s': nrm((N_GDN, DEC_BATCH, GDN_HEADS, GDN_DK, GDN_DV), 0.1),
        'state_gdn_conv': nrm((N_GDN, DEC_BATCH, CONV_WIDTH - 1, 3 * GDN_DIM)),
        'state_s5_re': nrm((N_S5, DEC_BATCH, S5_GROUPS, S5_STATE), 0.5),
        'state_s5_im': nrm((N_S5, DEC_BATCH, S5_GROUPS, S5_STATE), 0.5),
        'page_table': page_table,
        'c_prompt': nrm((BATCH, d)),
        'c_sample': nrm((DEC_BATCH, d)),
        'ada_w': nrm((DEPTH, d, 6 * d), 0.5 * d ** -0.5),
        'ada_b': nrm((DEPTH, 6 * d), 0.01),
        'norm_mix': 1.0 + nrm((DEPTH, d), 0.01),
        'norm_ffn': 1.0 + nrm((DEPTH, d), 0.01),
        'norm_final': 1.0 + nrm((d,), 0.01),
        'nsa_w_in': nrm((N_NSA, d, NSA_IN), d ** -0.5),
        'nsa_cmp_pe': nrm((N_NSA, 2, CMP_BLOCK, HEAD_DIM), 0.1),
        'nsa_cmp_w1': nrm((N_NSA, 2, CMP_BLOCK * HEAD_DIM, CMP_HIDDEN), (CMP_BLOCK * HEAD_DIM) ** -0.5),
        'nsa_cmp_w2': nrm((N_NSA, 2, CMP_HIDDEN, HEAD_DIM), CMP_HIDDEN ** -0.5),
        'nsa_w_out': nrm((N_NSA, NSA_Q_DIM, d), NSA_Q_DIM ** -0.5),
        'gdn_w_in': nrm((N_GDN, d, GDN_IN), d ** -0.5),
        'gdn_conv_w': nrm((N_GDN, CONV_WIDTH, 3 * GDN_DIM), CONV_WIDTH ** -0.5),
        'gdn_a_log': jnp.log(unif((N_GDN, GDN_HEADS), 1.0, 16.0)),
        'gdn_dt_bias': dt_g + jnp.log(-jnp.expm1(-dt_g)),
        'gdn_norm': 1.0 + nrm((N_GDN, GDN_DV), 0.01),
        'gdn_w_out': nrm((N_GDN, GDN_HEADS * GDN_DV, d), (GDN_HEADS * GDN_DV) ** -0.5),
        's5_w_in': nrm((N_S5, d, d), d ** -0.5),
        's5_lambda_re': -0.5 + nrm((N_S5, S5_GROUPS, S5_STATE), 0.01),
        's5_lambda_im': math.pi * s5_n + nrm((N_S5, S5_GROUPS, S5_STATE), 0.01),
        's5_b_re': nrm((N_S5, S5_GROUPS, S5_STATE, S5_GROUP), (2 * S5_GROUP) ** -0.5),
        's5_b_im': nrm((N_S5, S5_GROUPS, S5_STATE, S5_GROUP), (2 * S5_GROUP) ** -0.5),
        's5_c_re': nrm((N_S5, S5_GROUPS, S5_GROUP, S5_STATE), S5_STATE ** -0.5),
        's5_c_im': nrm((N_S5, S5_GROUPS, S5_GROUP, S5_STATE), S5_STATE ** -0.5),
        's5_d': nrm((N_S5, d)),
        's5_log_dt': unif((N_S5, S5_GROUPS), math.log(1e-3), math.log(1e-1)),
        's5_w_glu': nrm((N_S5, d, d), d ** -0.5),
        's5_w_out': nrm((N_S5, d, d), d ** -0.5),
        'ffn_w_gu': nrm((N_DENSE, d, 2 * D_FF), d ** -0.5),
        'ffn_w_down': nrm((N_DENSE, D_FF, d), D_FF ** -0.5),
        'moe_router': nrm((N_MOE, d, N_EXPERTS), d ** -0.5),
        'moe_w_gu': nrm((N_MOE, N_EXPERTS, d, 2 * D_FF_EXPERT), d ** -0.5),
        'moe_w_down': nrm((N_MOE, N_EXPERTS, D_FF_EXPERT, d), D_FF_EXPERT ** -0.5),
    }


def reference(x_prompt, x_sample, cache_nsa_kv, cache_nsa_win, state_gdn_s, state_gdn_conv,
              state_s5_re, state_s5_im, page_table, c_prompt, c_sample,
              ada_w, ada_b, norm_mix, norm_ffn, norm_final,
              nsa_w_in, nsa_cmp_pe, nsa_cmp_w1, nsa_cmp_w2, nsa_w_out,
              gdn_w_in, gdn_conv_w, gdn_a_log, gdn_dt_bias, gdn_norm, gdn_w_out,
              s5_w_in, s5_lambda_re, s5_lambda_im, s5_b_re, s5_b_im, s5_c_re, s5_c_im,
              s5_d, s5_log_dt, s5_w_glu, s5_w_out,
              ffn_w_gu, ffn_w_down, moe_router, moe_w_gu, moe_w_down):
    xp, xs = x_prompt, x_sample
    Bp = xp.shape[0]
    Ts = xs.shape[1]
    nsa_kv_p, nsa_kv_s, nsa_win_p, nsa_win_s = [], [], [], []
    gdn_s_p, gdn_s_s, gdn_conv_p, gdn_conv_s = [], [], [], []
    s5_re_p, s5_re_s, s5_im_p, s5_im_s = [], [], [], []
    for i in range(DEPTH):
        sh1_p, sc1_p, g1_p, sh2_p, sc2_p, g2_p = adaln_mods(c_prompt, ada_w[i], ada_b[i])
        sh1_s, sc1_s, g1_s, sh2_s, sc2_s, g2_s = adaln_mods(c_sample, ada_w[i], ada_b[i])
        hp = rmsnorm(xp, norm_mix[i]) * (1.0 + sc1_p) + sh1_p
        hs = rmsnorm(xs, norm_mix[i]) * (1.0 + sc1_s) + sh1_s
        j = i // N_MIXERS
        if i % N_MIXERS == 0:
            mp, kv_p, win_p = nsa_prompt(hp, nsa_w_in[j], nsa_cmp_pe[j], nsa_cmp_w1[j], nsa_cmp_w2[j], nsa_w_out[j])
            ms, kv_s, win_s = nsa_sample(hs, cache_nsa_kv[j], cache_nsa_win[j], page_table, nsa_w_in[j],
                                         nsa_cmp_pe[j], nsa_cmp_w1[j], nsa_cmp_w2[j], nsa_w_out[j])
            nsa_kv_p.append(kv_p); nsa_kv_s.append(kv_s)
            nsa_win_p.append(win_p); nsa_win_s.append(win_s)
        elif i % N_MIXERS == 1:
            buf0 = jnp.zeros((Bp, CONV_WIDTH - 1, 3 * GDN_DIM), hp.dtype)
            s00 = jnp.zeros((Bp, GDN_HEADS, GDN_DK, GDN_DV), jnp.float32)
            mp, cv_p, st_p = gdn_mixer(hp, buf0, s00, gdn_w_in[j], gdn_conv_w[j], gdn_a_log[j], gdn_dt_bias[j],
                                       gdn_norm[j], gdn_w_out[j], GDN_CHUNK)
            ms, cv_s, st_s = gdn_mixer(hs, state_gdn_conv[j], state_gdn_s[j], gdn_w_in[j], gdn_conv_w[j],
                                       gdn_a_log[j], gdn_dt_bias[j], gdn_norm[j], gdn_w_out[j], Ts)
            gdn_s_p.append(st_p.astype(state_gdn_s.dtype)); gdn_s_s.append(st_s.astype(state_gdn_s.dtype))
            gdn_conv_p.append(cv_p.astype(state_gdn_conv.dtype)); gdn_conv_s.append(cv_s.astype(state_gdn_conv.dtype))
        else:
            h00 = jnp.zeros((Bp, S5_GROUPS, S5_STATE), jnp.float32)
            mp, re_p, im_p = s5_mixer(hp, h00, h00, s5_w_in[j], s5_lambda_re[j], s5_lambda_im[j], s5_b_re[j],
                                      s5_b_im[j], s5_c_re[j], s5_c_im[j], s5_d[j], s5_log_dt[j], s5_w_glu[j], s5_w_out[j])
            ms, re_s, im_s = s5_mixer(hs, state_s5_re[j], state_s5_im[j], s5_w_in[j], s5_lambda_re[j], s5_lambda_im[j],
                                      s5_b_re[j], s5_b_im[j], s5_c_re[j], s5_c_im[j], s5_d[j], s5_log_dt[j],
                                      s5_w_glu[j], s5_w_out[j])
            s5_re_p.append(re_p.astype(state_s5_re.dtype)); s5_re_s.append(re_s.astype(state_s5_re.dtype))
            s5_im_p.append(im_p.astype(state_s5_im.dtype)); s5_im_s.append(im_s.astype(state_s5_im.dtype))
        xp = xp + g1_p * mp
        xs = xs + g1_s * ms
        hp = rmsnorm(xp, norm_ffn[i]) * (1.0 + sc2_p) + sh2_p
        hs = rmsnorm(xs, norm_ffn[i]) * (1.0 + sc2_s) + sh2_s
        f = i // 2
        if i % 2 == 0:
            fp = swiglu(hp, ffn_w_gu[f], ffn_w_down[f])
            fs = swiglu(hs, ffn_w_gu[f], ffn_w_down[f])
        else:
            fp = moe_swiglu(hp.reshape(-1, D_MODEL), moe_router[f], moe_w_gu[f], moe_w_down[f]).reshape(hp.shape)
            fs = moe_swiglu(hs.reshape(-1, D_MODEL), moe_router[f], moe_w_gu[f], moe_w_down[f]).reshape(hs.shape)
        xp = xp + g2_p * fp
        xs = xs + g2_s * fs
    y_prompt = rmsnorm(xp, norm_final)
    y_sample = rmsnorm(xs, norm_final)
    new_nsa_kv_prompt = jnp.stack(nsa_kv_p)
    new_nsa_kv_sample = jnp.stack(nsa_kv_s)
    new_nsa_win_prompt = jnp.stack(nsa_win_p)
    new_nsa_win_sample = jnp.stack(nsa_win_s)
    new_gdn_s_prompt = jnp.stack(gdn_s_p)
    new_gdn_s_sample = jnp.stack(gdn_s_s)
    new_gdn_conv_prompt = jnp.stack(gdn_conv_p)
    new_gdn_conv_sample = jnp.stack(gdn_conv_s)
    new_s5_re_prompt = jnp.stack(s5_re_p)
    new_s5_re_sample = jnp.stack(s5_re_s)
    new_s5_im_prompt = jnp.stack(s5_im_p)
    new_s5_im_sample = jnp.stack(s5_im_s)
    return (y_prompt, y_sample, new_nsa_kv_prompt, new_nsa_kv_sample, new_nsa_win_prompt, new_nsa_win_sample,
            new_gdn_s_prompt, new_gdn_s_sample, new_gdn_conv_prompt, new_gdn_conv_sample,
            new_s5_re_prompt, new_s5_re_sample, new_s5_im_prompt, new_s5_im_sample)
```

```python
import functools
import math

import jax
import jax.numpy as jnp
from jax import lax
from jax.experimental import pallas as pl
from jax.experimental.pallas import tpu as pltpu

F32 = jnp.float32
BF16 = jnp.bfloat16

D_MODEL = 1024
DEPTH = 4
PAGE_SIZE = 128
N_MIXERS = 3

NSA_HEADS = 16
HEAD_DIM = D_MODEL // NSA_HEADS
NSA_KV_HEADS = 2
NSA_HPG = NSA_HEADS // NSA_KV_HEADS
CMP_BLOCK = 32
CMP_STRIDE = 16
SEL_BLOCK = 64
SEL_TOPK = 16
WINDOW = 512
NSA_Q_BLOCK = 128
NSA_Q_DIM = NSA_HEADS * HEAD_DIM
NSA_KV_DIM = 6 * NSA_KV_HEADS * HEAD_DIM

GDN_HEADS = 8
GDN_DK = 128
GDN_DV = 128
GDN_DIM = GDN_HEADS * GDN_DK
CONV_WIDTH = 4
GDN_CHUNK = 64

S5_GROUP = 16
S5_GROUPS = D_MODEL // S5_GROUP
S5_STATE = 64

D_FF = 2816
N_EXPERTS = 8
TOP_K = 2
D_FF_EXPERT = 3584

RMS_EPS = 1e-6
NEG_INF = -1e30

LANE = 128
VMEM_LIMIT_BYTES = 56 * 1024 * 1024


def _cparams(*sem):
    return pltpu.CompilerParams(dimension_semantics=sem, vmem_limit_bytes=VMEM_LIMIT_BYTES)


def _round_up(n, m):
    return -(-n // m) * m


def _norm_mod(x, nw, sc, sh):
    y = x * lax.rsqrt(jnp.mean(x * x, axis=-1, keepdims=True) + RMS_EPS)
    return (y * nw) * (1.0 + sc) + sh


def _mod_spec(mod, tm, rows_per_batch):
    if mod.shape[1] == 1:
        return pl.BlockSpec((1, 1, mod.shape[2]), lambda i, *_: (i * tm // rows_per_batch, 0, 0))
    return pl.BlockSpec((1, tm, mod.shape[2]), lambda i, *_: (0, i, 0))


def _adaln_kernel(c_ref, w_ref, b_ref, o_ref):
    c = c_ref[...]
    s = c / (1.0 + jnp.exp(-c))
    o_ref[0] = jnp.dot(s, w_ref[0], preferred_element_type=F32, precision=lax.Precision.HIGHEST) + b_ref[0]


def adaln_all(c, ada_w, ada_b):
    r, d = c.shape
    n = ada_w.shape[2]
    tn = 1536
    return pl.pallas_call(
        _adaln_kernel,
        grid=(DEPTH, n // tn),
        in_specs=[pl.BlockSpec((r, d), lambda l, j: (0, 0)),
                  pl.BlockSpec((1, d, tn), lambda l, j: (l, 0, j)),
                  pl.BlockSpec((1, 1, tn), lambda l, j: (l, 0, j))],
        out_specs=pl.BlockSpec((1, r, tn), lambda l, j: (l, 0, j)),
        out_shape=jax.ShapeDtypeStruct((DEPTH, r, n), F32),
        compiler_params=_cparams("arbitrary", "arbitrary"),
        name="adaln",
    )(c, ada_w, ada_b.reshape(DEPTH, 1, n))


def _dot(a, b):
    prec = lax.Precision.HIGHEST if b.dtype == F32 else None
    return jnp.dot(a.astype(b.dtype), b, preferred_element_type=F32, precision=prec)


def _ln_mm_kernel(x_ref, nw_ref, sc_ref, sh_ref, w_ref, o_ref, h_ref):
    @pl.when(pl.program_id(1) == 0)
    def _():
        h_ref[...] = _norm_mod(x_ref[...], nw_ref[...], sc_ref[0], sh_ref[0]).astype(h_ref.dtype)

    o_ref[...] = _dot(h_ref[...], w_ref[...])


def ln_matmul(x, nw, sc, sh, w, *, tm, tn, rows_per_batch):
    m, d = x.shape
    n = w.shape[1]
    return pl.pallas_call(
        _ln_mm_kernel,
        grid=(m // tm, n // tn),
        in_specs=[pl.BlockSpec((tm, d), lambda i, j: (i, 0)),
                  pl.BlockSpec((1, d), lambda i, j: (0, 0)),
                  _mod_spec(sc, tm, rows_per_batch),
                  _mod_spec(sh, tm, rows_per_batch),
                  pl.BlockSpec((d, tn), lambda i, j: (0, j))],
        out_specs=pl.BlockSpec((tm, tn), lambda i, j: (i, j)),
        out_shape=jax.ShapeDtypeStruct((m, n), F32),
        scratch_shapes=[pltpu.VMEM((tm, d), w.dtype)],
        compiler_params=_cparams("arbitrary", "arbitrary"),
        name="ln_matmul",
    )(x, nw.reshape(1, d), sc, sh, w)


def _mm_res_kernel(a_ref, w_ref, x_ref, g_ref, o_ref):
    o_ref[...] = x_ref[...] + g_ref[0] * _dot(a_ref[...], w_ref[...])


def matmul_residual(a, w, x, g, *, tm, rows_per_batch):
    m, k = a.shape
    n = w.shape[1]
    return pl.pallas_call(
        _mm_res_kernel,
        grid=(m // tm,),
        in_specs=[pl.BlockSpec((tm, k), lambda i: (i, 0)),
                  pl.BlockSpec((k, n), lambda i: (0, 0)),
                  pl.BlockSpec((tm, n), lambda i: (i, 0)),
                  _mod_spec(g, tm, rows_per_batch)],
        out_specs=pl.BlockSpec((tm, n), lambda i: (i, 0)),
        out_shape=jax.ShapeDtypeStruct((m, n), F32),
        compiler_params=_cparams("arbitrary"),
        name="matmul_residual",
    )(a, w, x, g)


def _silu(a):
    return a / (1.0 + jnp.exp(-a))


def _ffn_kernel(x_ref, nw_ref, sc_ref, sh_ref, g_ref, wa_ref, wb_ref, wd_ref, o_ref, h_ref, acc_ref):
    j = pl.program_id(1)

    @pl.when(j == 0)
    def _():
        h_ref[...] = _norm_mod(x_ref[...], nw_ref[...], sc_ref[0], sh_ref[0]).astype(h_ref.dtype)
        acc_ref[...] = jnp.zeros_like(acc_ref)

    h = h_ref[...]
    a = _dot(h, wa_ref[...])
    b = _dot(h, wb_ref[...])
    acc_ref[...] += _dot(_silu(a) * b, wd_ref[...])

    @pl.when(j == pl.num_programs(1) - 1)
    def _():
        o_ref[...] = x_ref[...] + g_ref[0] * acc_ref[...]


def dense_ffn(x, nw, sc, sh, g, w_gu, w_down, *, tm, tf, rows_per_batch):
    m, d = x.shape
    ff = w_down.shape[0]
    nj = ff // tf
    return pl.pallas_call(
        _ffn_kernel,
        grid=(m // tm, nj),
        in_specs=[pl.BlockSpec((tm, d), lambda i, j: (i, 0)),
                  pl.BlockSpec((1, d), lambda i, j: (0, 0)),
                  _mod_spec(sc, tm, rows_per_batch),
                  _mod_spec(sh, tm, rows_per_batch),
                  _mod_spec(g, tm, rows_per_batch),
                  pl.BlockSpec((d, tf), lambda i, j: (0, j)),
                  pl.BlockSpec((d, tf), lambda i, j: (0, nj + j)),
                  pl.BlockSpec((tf, d), lambda i, j: (j, 0))],
        out_specs=pl.BlockSpec((tm, d), lambda i, j: (i, 0)),
        out_shape=jax.ShapeDtypeStruct((m, d), F32),
        scratch_shapes=[pltpu.VMEM((tm, d), w_gu.dtype), pltpu.VMEM((tm, d), F32)],
        compiler_params=_cparams("arbitrary", "arbitrary"),
        name="dense_ffn",
    )(x, nw.reshape(1, d), sc, sh, g, w_gu, w_gu, w_down)


def _ln_router_kernel(x_ref, nw_ref, sc_ref, sh_ref, wr_ref, h_ref, lg_ref):
    h = _norm_mod(x_ref[...], nw_ref[...], sc_ref[0], sh_ref[0])
    h_ref[...] = h.astype(h_ref.dtype)
    lg_ref[...] = jnp.dot(h, wr_ref[...], preferred_element_type=F32, precision=lax.Precision.HIGHEST)


def ln_router(x, nw, sc, sh, w_router, *, tm, rows_per_batch, h_dtype):
    m, d = x.shape
    wr = jnp.pad(w_router, ((0, 0), (0, LANE - N_EXPERTS)))
    h, lg = pl.pallas_call(
        _ln_router_kernel,
        grid=(m // tm,),
        in_specs=[pl.BlockSpec((tm, d), lambda i: (i, 0)),
                  pl.BlockSpec((1, d), lambda i: (0, 0)),
                  _mod_spec(sc, tm, rows_per_batch),
                  _mod_spec(sh, tm, rows_per_batch),
                  pl.BlockSpec((d, LANE), lambda i: (0, 0))],
        out_specs=[pl.BlockSpec((tm, d), lambda i: (i, 0)),
                   pl.BlockSpec((tm, LANE), lambda i: (i, 0))],
        out_shape=[jax.ShapeDtypeStruct((m, d), h_dtype), jax.ShapeDtypeStruct((m, LANE), F32)],
        compiler_params=_cparams("arbitrary"),
        name="ln_router",
    )(x, nw.reshape(1, d), sc, sh, wr)
    return h, lg[:, :N_EXPERTS]


def _moe_kernel(be_ref, nu_ref, r_ref, wa_ref, wb_ref, wd_ref, o_ref, acc_ref):
    i = pl.program_id(0)
    j = pl.program_id(1)

    @pl.when(i < nu_ref[0])
    def _():
        @pl.when(j == 0)
        def _():
            acc_ref[...] = jnp.zeros_like(acc_ref)

        r = r_ref[...]
        a = _dot(r, wa_ref[0])
        b = _dot(r, wb_ref[0])
        acc_ref[...] += _dot(_silu(a) * b, wd_ref[0])

        @pl.when(j == pl.num_programs(1) - 1)
        def _():
            o_ref[...] = acc_ref[...]


def moe_experts(rows, block_e, n_used, w_gu, w_down, *, tm, tf):
    m, d = rows.shape
    ff = w_down.shape[1]
    nj = ff // tf
    n_blocks = m // tm

    def blk(i, nu):
        return jnp.minimum(i, nu[0] - 1)

    def jj(i, j, nu):
        return jnp.where(i < nu[0], j, nj - 1)

    grid_spec = pltpu.PrefetchScalarGridSpec(
        num_scalar_prefetch=2,
        grid=(n_blocks, nj),
        in_specs=[pl.BlockSpec((tm, d), lambda i, j, be, nu: (blk(i, nu), 0)),
                  pl.BlockSpec((1, d, tf), lambda i, j, be, nu: (be[blk(i, nu)], 0, jj(i, j, nu))),
                  pl.BlockSpec((1, d, tf), lambda i, j, be, nu: (be[blk(i, nu)], 0, nj + jj(i, j, nu))),
                  pl.BlockSpec((1, tf, d), lambda i, j, be, nu: (be[blk(i, nu)], jj(i, j, nu), 0))],
        out_specs=pl.BlockSpec((tm, d), lambda i, j, be, nu: (blk(i, nu), 0)),
        scratch_shapes=[pltpu.VMEM((tm, d), F32)],
    )
    return pl.pallas_call(
        _moe_kernel,
        grid_spec=grid_spec,
        out_shape=jax.ShapeDtypeStruct((m, d), F32),
        compiler_params=_cparams("arbitrary", "arbitrary"),
        name="moe_experts",
    )(block_e, n_used, rows, w_gu, w_gu, w_down)


def moe_layer(x, nw, sc, sh, g, w_router, w_gu, w_down, *, tm_ln, tm, tf, rows_per_batch):
    n_tok, d = x.shape
    n_rows = n_tok * TOP_K
    h, logits = ln_router(x, nw, sc, sh, w_router, tm=tm_ln, rows_per_batch=rows_per_batch, h_dtype=w_gu.dtype)
    top_logit, top_e = lax.top_k(logits, TOP_K)
    gate = jax.nn.softmax(top_logit, axis=-1)
    flat_e = top_e.reshape(-1)
    order = jnp.argsort(flat_e)
    e_sorted = flat_e[order]
    tok_sorted = order // TOP_K
    counts = jnp.bincount(flat_e, length=N_EXPERTS)
    padded = (counts + tm - 1) // tm * tm
    pad_end = jnp.cumsum(padded)
    pad_start = pad_end - padded
    grp_start = jnp.cumsum(counts) - counts
    dest = pad_start[e_sorted] + jnp.arange(n_rows) - grp_start[e_sorted]
    n_blocks = -(-n_rows // tm) + N_EXPERTS
    rows = jnp.zeros((n_blocks * tm, d), h.dtype).at[dest].set(h[tok_sorted])
    block_e = jnp.minimum(jnp.searchsorted(pad_end, jnp.arange(n_blocks) * tm, side='right'),
                          N_EXPERTS - 1).astype(jnp.int32)
    n_used = (pad_end[-1] // tm).astype(jnp.int32).reshape(1)
    out = moe_experts(rows, block_e, n_used, w_gu, w_down, tm=tm, tf=tf)
    pos = jnp.zeros((n_rows,), jnp.int32).at[order].set(dest.astype(jnp.int32)).reshape(n_tok, TOP_K)
    f = out[pos[:, 0]] * gate[:, 0:1] + out[pos[:, 1]] * gate[:, 1:2]
    if g.shape[1] == 1:
        gg = jnp.repeat(g[:, 0], rows_per_batch, axis=0)
    else:
        gg = g[0]
    return x + gg * f


def alibi_slopes():
    return jnp.exp2(-8.0 * (jnp.arange(NSA_HEADS, dtype=F32) + 1.0) / NSA_HEADS)


def nsa_split(proj, B, T):
    q = proj[..., :NSA_Q_DIM].reshape(B, T, NSA_HEADS, HEAD_DIM)
    kv = proj[..., NSA_Q_DIM:NSA_Q_DIM + NSA_KV_DIM].reshape(B, T, 6, NSA_KV_HEADS, HEAD_DIM)
    gates = jax.nn.sigmoid(proj[..., NSA_Q_DIM + NSA_KV_DIM:NSA_Q_DIM + NSA_KV_DIM + 3 * NSA_HEADS]
                           ).reshape(B, T, NSA_HEADS, 3)
    return q, kv, gates


def nsa_compress(seq, pe, w1, w2):
    B, L, G, dh = seq.shape
    r = CMP_BLOCK // CMP_STRIDE
    n_chunk = L // CMP_STRIDE
    nc = n_chunk - r + 1
    ch = seq.reshape(B, n_chunk, CMP_STRIDE, G, dh)
    blocks = jnp.concatenate([ch[:, j:j + nc] for j in range(r)], axis=2)
    blocks = blocks + pe[None, None, :, None, :]
    flat = blocks.transpose(0, 1, 3, 2, 4).reshape(B, nc, G, CMP_BLOCK * dh)
    return jax.nn.gelu(flat @ w1) @ w2


def cmp_to_sel_map(nc, nsb):
    c_start = jnp.arange(nc) * CMP_STRIDE
    s_start = jnp.arange(nsb) * SEL_BLOCK
    hit = (c_start[:, None] < s_start[None, :] + SEL_BLOCK) & (c_start[:, None] + CMP_BLOCK > s_start[None, :])
    return hit.astype(F32)


def nsa_attend_block(q, gates, q_pos, kc, vc, kc_pos, ks_t, vs_t, kw, vw, kw_pos):
    B, Tq = q.shape[:2]
    nc = kc.shape[1]
    nsb = ks_t.shape[2]
    n_sel = min(SEL_TOPK, nsb)
    slopes = alibi_slopes().reshape(NSA_KV_HEADS, NSA_HPG)
    qg = q.reshape(B, Tq, NSA_KV_HEADS, NSA_HPG, HEAD_DIM) * (HEAD_DIM ** -0.5)

    d_c = q_pos[:, None] - kc_pos[None, :]
    ok_c = (d_c >= 0)[None, :, None, None, :]
    s_c = jnp.einsum('btghd,bngd->btghn', qg, kc).astype(F32)
    s_c = s_c - slopes[None, None, :, :, None] * jnp.abs(d_c).astype(F32)[None, :, None, None, :]
    s_c = jnp.where(ok_c, s_c, NEG_INF)
    p_c = jax.nn.softmax(s_c, axis=-1) * ok_c
    o_c = jnp.einsum('btghn,bngd->btghd', p_c.astype(vc.dtype), vc)

    imp = jnp.einsum('btghn,nj->btgj', p_c, cmp_to_sel_map(nc, nsb))
    cur = q_pos // SEL_BLOCK
    blk = jnp.arange(nsb)
    is_cur = (blk[None, :] == cur[:, None])[None, :, None, :]
    is_past = (blk[None, :] < cur[:, None])[None, :, None, :]
    imp = jnp.where(is_cur, jnp.inf, jnp.where(is_past, imp, -jnp.inf))
    _, idx = lax.top_k(imp, n_sel)
    bi = jnp.arange(B)[:, None, None, None]
    gi = jnp.arange(NSA_KV_HEADS)[None, None, :, None]
    k_sel = ks_t[bi, gi, idx]
    v_sel = vs_t[bi, gi, idx]
    pos_s = idx[..., None] * SEL_BLOCK + jnp.arange(SEL_BLOCK)
    d_s = q_pos[None, :, None, None, None] - pos_s
    ok_s = (d_s >= 0)[:, :, :, None]
    s_s = jnp.einsum('btghd,btgnkd->btghnk', qg, k_sel).astype(F32)
    s_s = s_s - slopes[None, None, :, :, None, None] * jnp.abs(d_s).astype(F32)[:, :, :, None]
    s_s = jnp.where(ok_s, s_s, NEG_INF).reshape(B, Tq, NSA_KV_HEADS, NSA_HPG, n_sel * SEL_BLOCK)
    p_s = jax.nn.softmax(s_s, axis=-1).reshape(B, Tq, NSA_KV_HEADS, NSA_HPG, n_sel, SEL_BLOCK)
    o_s = jnp.einsum('btghnk,btgnkd->btghd', p_s.astype(v_sel.dtype), v_sel)

    d_w = q_pos[:, None] - kw_pos[None, :]
    ok_w = ((d_w >= 0) & (d_w < WINDOW) & (kw_pos >= 0)[None, :])[None, :, None, None, :]
    s_w = jnp.einsum('btghd,blgd->btghl', qg, kw).astype(F32)
    s_w = s_w - slopes[None, None, :, :, None] * jnp.abs(d_w).astype(F32)[None, :, None, None, :]
    p_w = jax.nn.softmax(jnp.where(ok_w, s_w, NEG_INF), axis=-1)
    o_w = jnp.einsum('btghl,blgd->btghd', p_w.astype(vw.dtype), vw)

    g = gates.reshape(B, Tq, NSA_KV_HEADS, NSA_HPG, 3).astype(q.dtype)
    o = g[..., 0:1] * o_c + g[..., 1:2] * o_s + g[..., 2:3] * o_w
    return o.reshape(B, Tq, NSA_Q_DIM)


def nsa_prompt_core(proj, B, T, pe, w1, w2):
    q, kv, gates = nsa_split(proj, B, T)
    kc = nsa_compress(kv[:, :, 0], pe[0], w1[0], w2[0])
    vc = nsa_compress(kv[:, :, 1], pe[1], w1[1], w2[1])
    kc_pos = jnp.arange(kc.shape[1]) * CMP_STRIDE + CMP_BLOCK - 1
    nsb = T // SEL_BLOCK
    ks_t = kv[:, :, 2].reshape(B, nsb, SEL_BLOCK, NSA_KV_HEADS, HEAD_DIM).transpose(0, 3, 1, 2, 4)
    vs_t = kv[:, :, 3].reshape(B, nsb, SEL_BLOCK, NSA_KV_HEADS, HEAD_DIM).transpose(0, 3, 1, 2, 4)
    win_pad = jnp.pad(kv[:, :, 4:], ((0, 0), (WINDOW, 0), (0, 0), (0, 0), (0, 0)))

    def q_block(i):
        start = i * NSA_Q_BLOCK
        qb = lax.dynamic_slice_in_dim(q, start, NSA_Q_BLOCK, axis=1)
        gb = lax.dynamic_slice_in_dim(gates, start, NSA_Q_BLOCK, axis=1)
        wb = lax.dynamic_slice_in_dim(win_pad, start, WINDOW + NSA_Q_BLOCK, axis=1)
        q_pos = start + jnp.arange(NSA_Q_BLOCK)
        kw_pos = start - WINDOW + jnp.arange(WINDOW + NSA_Q_BLOCK)
        return nsa_attend_block(qb, gb, q_pos, kc, vc, kc_pos, ks_t, vs_t, wb[:, :, 0], wb[:, :, 1], kw_pos)

    o = lax.map(q_block, jnp.arange(T // NSA_Q_BLOCK))
    o = o.transpose(1, 0, 2, 3).reshape(B, T, NSA_Q_DIM)
    return o, kv[:, :, :4], kv[:, T - min(WINDOW, T):, 4:]


def nsa_sample_core(proj, B, T, cache_kv, cache_win, page_table, pe, w1, w2):
    past_len = page_table.shape[1] * cache_kv.shape[1]
    win_len = cache_win.shape[1]
    q, kv, gates = nsa_split(proj, B, T)
    past = cache_kv[page_table].reshape(B, past_len, 4, NSA_KV_HEADS, HEAD_DIM)
    full = jnp.concatenate([past, kv[:, :, :4]], axis=1)
    L = past_len + T
    Lp = -(-L // SEL_BLOCK) * SEL_BLOCK
    full = jnp.pad(full, ((0, 0), (0, Lp - L), (0, 0), (0, 0), (0, 0)))
    kc = nsa_compress(full[:, :, 0], pe[0], w1[0], w2[0])
    vc = nsa_compress(full[:, :, 1], pe[1], w1[1], w2[1])
    kc_pos = jnp.arange(kc.shape[1]) * CMP_STRIDE + CMP_BLOCK - 1
    nsb = Lp // SEL_BLOCK
    ks_t = full[:, :, 2].reshape(B, nsb, SEL_BLOCK, NSA_KV_HEADS, HEAD_DIM).transpose(0, 3, 1, 2, 4)
    vs_t = full[:, :, 3].reshape(B, nsb, SEL_BLOCK, NSA_KV_HEADS, HEAD_DIM).transpose(0, 3, 1, 2, 4)
    win = jnp.concatenate([cache_win, kv[:, :, 4:]], axis=1)
    kw_pos = past_len - win_len + jnp.arange(win_len + T)
    q_pos = past_len + jnp.arange(T)
    o = nsa_attend_block(q, gates, q_pos, kc, vc, kc_pos, ks_t, vs_t, win[:, :, 0], win[:, :, 1], kw_pos)
    return o, kv[:, :, :4], win[:, win.shape[1] - win_len:]


def causal_dwconv(xp, w):
    return lax.conv_general_dilated(xp, w[:, None, :], window_strides=(1,), padding='VALID',
                                    dimension_numbers=('NWC', 'WIO', 'NWC'),
                                    feature_group_count=xp.shape[-1])


def l2norm(x):
    return x * lax.rsqrt(jnp.sum(x * x, axis=-1, keepdims=True) + 1e-6)


def gated_delta_chunked(q, k, v, g, beta, s0, chunk):
    B, T, H, DK = q.shape
    DV = v.shape[-1]
    n = T // chunk

    def blk(a):
        return jnp.moveaxis(a.reshape(B, n, chunk, H, *a.shape[3:]), 3, 2)

    q, k, v, g, beta = blk(q), blk(k), blk(v), blk(g), blk(beta)
    gc = jnp.cumsum(g, axis=-1)
    causal = jnp.tril(jnp.ones((chunk, chunk), dtype=bool))
    strict = jnp.tril(jnp.ones((chunk, chunk), dtype=bool), -1)
    diff = gc[..., :, None] - gc[..., None, :]
    decay = jnp.where(causal, jnp.exp(jnp.where(causal, diff, 0.0)), 0.0)
    kk = jnp.einsum('bnhik,bnhjk->bnhij', k, k)
    lower = jnp.where(strict, beta[..., :, None] * kk * decay, 0.0)
    a_mat = lower + jnp.eye(chunk, dtype=lower.dtype)
    rhs = jnp.concatenate([v * beta[..., None], k * (beta * jnp.exp(gc))[..., None]], axis=-1)
    sol = lax.linalg.triangular_solve(a_mat, rhs, left_side=True, lower=True, unit_diagonal=True)
    u, w = sol[..., :DV], sol[..., DV:]
    qk = jnp.einsum('bnhik,bnhjk->bnhij', q, k) * decay
    q_dec = q * jnp.exp(gc)[..., None]
    k_dec = k * jnp.exp(gc[..., -1:] - gc)[..., None]
    g_last = jnp.exp(gc[..., -1])

    def step(s, xs):
        u_c, w_c, qk_c, qd_c, kd_c, gl_c = xs
        v_new = u_c - jnp.einsum('bhck,bhkv->bhcv', w_c, s)
        o_c = jnp.einsum('bhck,bhkv->bhcv', qd_c, s) + jnp.einsum('bhij,bhjv->bhiv', qk_c, v_new)
        s = s * gl_c[..., None, None] + jnp.einsum('bhck,bhcv->bhkv', kd_c, v_new)
        return s, o_c

    xs = tuple(jnp.moveaxis(a, 1, 0) for a in (u, w, qk, q_dec, k_dec, g_last))
    s, o = lax.scan(step, s0, xs)
    o = jnp.moveaxis(jnp.moveaxis(o, 0, 1), 2, 3).reshape(B, T, H, DV)
    return o, s


def gdn_core(proj, B, T, conv_buf, s0, conv_w, a_log, dt_bias, norm_w, chunk):
    qkv_raw = proj[..., :3 * GDN_DIM]
    z = proj[..., 3 * GDN_DIM:4 * GDN_DIM].reshape(B, T, GDN_HEADS, GDN_DV)
    a = proj[..., 4 * GDN_DIM:4 * GDN_DIM + GDN_HEADS]
    b = proj[..., 4 * GDN_DIM + GDN_HEADS:4 * GDN_DIM + 2 * GDN_HEADS]
    xpad = jnp.concatenate([conv_buf, qkv_raw], axis=1)
    qkv = jax.nn.silu(causal_dwconv(xpad, conv_w))
    q, k, v = jnp.split(qkv, 3, axis=-1)
    q = l2norm(q.reshape(B, T, GDN_HEADS, GDN_DK)) * (GDN_DK ** -0.5)
    k = l2norm(k.reshape(B, T, GDN_HEADS, GDN_DK))
    v = v.reshape(B, T, GDN_HEADS, GDN_DV)
    beta = jax.nn.sigmoid(b)
    g = -jnp.exp(a_log) * jax.nn.softplus(a + dt_bias)
    o, s = gated_delta_chunked(q, k, v, g, beta, s0, chunk)
    of = o * lax.rsqrt(jnp.mean(o * o, axis=-1, keepdims=True) + RMS_EPS) * norm_w
    o = of * jax.nn.silu(z)
    return o.reshape(B, T, GDN_HEADS * GDN_DV), xpad[:, xpad.shape[1] - (CONV_WIDTH - 1):], s


def complex_linear_combine(e1, e2):
    a1r, a1i, b1r, b1i = e1
    a2r, a2i, b2r, b2i = e2
    return (a2r * a1r - a2i * a1i, a2r * a1i + a2i * a1r,
            a2r * b1r - a2i * b1i + b2r, a2r * b1i + a2i * b1r + b2i)


def s5_core(u, B, T, h0r, h0i, lam_re, lam_im, b_re, b_im, c_re, c_im, d_skip, log_dt, w_glu):
    ug = u.reshape(B, T, S5_GROUPS, S5_GROUP)
    dt = jnp.exp(log_dt)[:, None]
    lr, li = lam_re, lam_im
    mag = jnp.exp(lr * dt)
    ar, ai = mag * jnp.cos(li * dt), mag * jnp.sin(li * dt)
    den = lr * lr + li * li
    fr = ((ar - 1.0) * lr + ai * li) / den
    fi = (ai * lr - (ar - 1.0) * li) / den
    bbar_re = fr[..., None] * b_re - fi[..., None] * b_im
    bbar_im = fr[..., None] * b_im + fi[..., None] * b_re
    bu_re = jnp.einsum('gpc,btgc->btgp', bbar_re, ug)
    bu_im = jnp.einsum('gpc,btgc->btgp', bbar_im, ug)
    bu_re = bu_re.at[:, 0].add(ar * h0r - ai * h0i)
    bu_im = bu_im.at[:, 0].add(ar * h0i + ai * h0r)
    a_re = jnp.broadcast_to(ar, bu_re.shape)
    a_im = jnp.broadcast_to(ai, bu_im.shape)
    _, _, hr, hi = lax.associative_scan(complex_linear_combine, (a_re, a_im, bu_re, bu_im), axis=1)
    y = jnp.einsum('gcp,btgp->btgc', c_re, hr) - jnp.einsum('gcp,btgp->btgc', c_im, hi)
    y = y.reshape(B, T, D_MODEL) + d_skip * u.reshape(B, T, D_MODEL)
    z = jax.nn.gelu(y)
    z = z * jax.nn.sigmoid(z @ w_glu)
    return z, hr[:, -1], hi[:, -1]


def _pad_cols(w, n):
    return jnp.pad(w, ((0, 0), (0, n - w.shape[1])))


def kernel(x_prompt, x_sample, cache_nsa_kv, cache_nsa_win, state_gdn_s, state_gdn_conv, state_s5_re, state_s5_im, page_table, c_prompt, c_sample, ada_w, ada_b, norm_mix, norm_ffn, norm_final, nsa_w_in, nsa_cmp_pe, nsa_cmp_w1, nsa_cmp_w2, nsa_w_out, gdn_w_in, gdn_conv_w, gdn_a_log, gdn_dt_bias, gdn_norm, gdn_w_out, s5_w_in, s5_lambda_re, s5_lambda_im, s5_b_re, s5_b_im, s5_c_re, s5_c_im, s5_d, s5_log_dt, s5_w_glu, s5_w_out, ffn_w_gu, ffn_w_down, moe_router, moe_w_gu, moe_w_down):
    Bp, Tp, d = x_prompt.shape
    Bs, Ts, _ = x_sample.shape
    Mp, Ms = Bp * Tp, Bs * Ts
    xp = x_prompt.reshape(Mp, d)
    xs = x_sample.reshape(Ms, d)

    c_all = jnp.concatenate([c_prompt, c_sample], axis=0)
    r_pad = _round_up(c_all.shape[0], 8)
    mods = adaln_all(jnp.pad(c_all, ((0, r_pad - c_all.shape[0]), (0, 0))), ada_w, ada_b)

    def mods_of(i):
        parts = jnp.split(mods[i], 6, axis=-1)
        mp = [p[:Bp].reshape(Bp, 1, d) for p in parts]
        ms = [jnp.repeat(p[Bp:Bp + Bs], Ts, axis=0).reshape(1, Ms, d) for p in parts]
        return mp, ms

    P = dict(rows_per_batch=Tp)
    S = dict(rows_per_batch=Ts)

    nsa_kv_p, nsa_kv_s, nsa_win_p, nsa_win_s = [], [], [], []
    gdn_s_p, gdn_s_s, gdn_conv_p, gdn_conv_s = [], [], [], []
    s5_re_p, s5_re_s, s5_im_p, s5_im_s = [], [], [], []

    HI = "highest"
    for i in range(DEPTH):
        (sh1_p, sc1_p, g1_p, sh2_p, sc2_p, g2_p), (sh1_s, sc1_s, g1_s, sh2_s, sc2_s, g2_s) = mods_of(i)
        j = i // N_MIXERS
        if i % N_MIXERS == 0:
            n_in = _round_up(nsa_w_in.shape[2], LANE)
            w_in = _pad_cols(nsa_w_in[j], n_in)
            w_out = nsa_w_out[j]
            proj_p = ln_matmul(xp, norm_mix[i], sc1_p, sh1_p, w_in.astype(BF16), tm=512, tn=n_in, **P)
            proj_s = ln_matmul(xs, norm_mix[i], sc1_s, sh1_s, w_in, tm=Ms, tn=n_in // 3, **S)
            o_p, kv_p, win_p = nsa_prompt_core(proj_p.reshape(Bp, Tp, n_in), Bp, Tp,
                                               nsa_cmp_pe[j], nsa_cmp_w1[j], nsa_cmp_w2[j])
            with jax.default_matmul_precision(HI):
                o_s, kv_s, win_s = nsa_sample_core(proj_s.reshape(Bs, Ts, n_in), Bs, Ts, cache_nsa_kv[j],
                                                   cache_nsa_win[j], page_table,
                                                   nsa_cmp_pe[j], nsa_cmp_w1[j], nsa_cmp_w2[j])
            nsa_kv_p.append(kv_p); nsa_kv_s.append(kv_s)
            nsa_win_p.append(win_p); nsa_win_s.append(win_s)
        elif i % N_MIXERS == 1:
            n_in = _round_up(gdn_w_in.shape[2], LANE)
            w_in = _pad_cols(gdn_w_in[j], n_in)
            w_out = gdn_w_out[j]
            proj_p = ln_matmul(xp, norm_mix[i], sc1_p, sh1_p, w_in.astype(BF16), tm=256, tn=n_in, **P)
            proj_s = ln_matmul(xs, norm_mix[i], sc1_s, sh1_s, w_in, tm=Ms, tn=n_in // 3, **S)
            buf0 = jnp.zeros((Bp, CONV_WIDTH - 1, 3 * GDN_DIM), F32)
            s00 = jnp.zeros((Bp, GDN_HEADS, GDN_DK, GDN_DV), F32)
            o_p, cv_p, st_p = gdn_core(proj_p.reshape(Bp, Tp, n_in), Bp, Tp, buf0, s00, gdn_conv_w[j],
                                       gdn_a_log[j], gdn_dt_bias[j], gdn_norm[j], GDN_CHUNK)
            with jax.default_matmul_precision(HI):
                o_s, cv_s, st_s = gdn_core(proj_s.reshape(Bs, Ts, n_in), Bs, Ts, state_gdn_conv[j],
                                           state_gdn_s[j], gdn_conv_w[j], gdn_a_log[j], gdn_dt_bias[j],
                                           gdn_norm[j], Ts)
            gdn_s_p.append(st_p); gdn_s_s.append(st_s)
            gdn_conv_p.append(cv_p); gdn_conv_s.append(cv_s)
        else:
            w_in = s5_w_in[j]
            w_out = s5_w_out[j]
            u_p = ln_matmul(xp, norm_mix[i], sc1_p, sh1_p, w_in.astype(BF16), tm=512, tn=d, **P)
            u_s = ln_matmul(xs, norm_mix[i], sc1_s, sh1_s, w_in, tm=Ms, tn=d, **S)
            h00 = jnp.zeros((Bp, S5_GROUPS, S5_STATE), F32)
            s5w = (s5_lambda_re[j], s5_lambda_im[j], s5_b_re[j], s5_b_im[j], s5_c_re[j], s5_c_im[j],
                   s5_d[j], s5_log_dt[j], s5_w_glu[j])
            o_p, re_p, im_p = s5_core(u_p.reshape(Bp, Tp, d), Bp, Tp, h00, h00, *s5w)
            with jax.default_matmul_precision(HI):
                o_s, re_s, im_s = s5_core(u_s.reshape(Bs, Ts, d), Bs, Ts, state_s5_re[j], state_s5_im[j], *s5w)
            s5_re_p.append(re_p); s5_re_s.append(re_s)
            s5_im_p.append(im_p); s5_im_s.append(im_s)
        xp = matmul_residual(o_p.reshape(Mp, -1), w_out.astype(BF16), xp, g1_p, tm=512, **P)
        xs = matmul_residual(o_s.reshape(Ms, -1), w_out, xs, g1_s, tm=Ms, **S)

        f = i // 2
        if i % 2 == 0:
            w_gu, w_down = ffn_w_gu[f], ffn_w_down[f]
            xp = dense_ffn(xp, norm_ffn[i], sc2_p, sh2_p, g2_p, w_gu.astype(BF16), w_down.astype(BF16),
                           tm=1024, tf=256, **P)
            xs = dense_ffn(xs, norm_ffn[i], sc2_s, sh2_s, g2_s, w_gu, w_down, tm=Ms, tf=256, **S)
        else:
            w_gu, w_down = moe_w_gu[f], moe_w_down[f]
            w_gu_b, w_down_b = w_gu.astype(BF16), w_down.astype(BF16)
            xp = moe_layer(xp, norm_ffn[i], sc2_p, sh2_p, g2_p, moe_router[f], w_gu_b, w_down_b,
                           tm_ln=512, tm=512, tf=512, **P)
            last = i == DEPTH - 1
            xs = moe_layer(xs, norm_ffn[i], sc2_s, sh2_s, g2_s, moe_router[f],
                           w_gu_b if last else w_gu, w_down_b if last else w_down,
                           tm_ln=Ms, tm=128, tf=512, **S)

    def final_norm(x):
        return x * lax.rsqrt(jnp.mean(x * x, axis=-1, keepdims=True) + RMS_EPS) * norm_final

    y_prompt = final_norm(xp).reshape(Bp, Tp, d)
    y_sample = final_norm(xs).reshape(Bs, Ts, d)
    return (y_prompt, y_sample, jnp.stack(nsa_kv_p), jnp.stack(nsa_kv_s), jnp.stack(nsa_win_p),
            jnp.stack(nsa_win_s), jnp.stack(gdn_s_p), jnp.stack(gdn_s_s), jnp.stack(gdn_conv_p),
            jnp.stack(gdn_conv_s), jnp.stack(s5_re_p), jnp.stack(s5_re_s), jnp.stack(s5_im_p),
            jnp.stack(s5_im_s))
```

```python
import functools
import math

import jax
import jax.numpy as jnp
from jax import lax
from jax.experimental import pallas as pl
from jax.experimental.pallas import tpu as pltpu

F32 = jnp.float32
BF16 = jnp.bfloat16

D_MODEL = 1024
DEPTH = 4
PAGE_SIZE = 128
N_MIXERS = 3

NSA_HEADS = 16
HEAD_DIM = D_MODEL // NSA_HEADS
NSA_KV_HEADS = 2
NSA_HPG = NSA_HEADS // NSA_KV_HEADS
CMP_BLOCK = 32
CMP_STRIDE = 16
SEL_BLOCK = 64
SEL_TOPK = 16
WINDOW = 512
NSA_Q_BLOCK = 128
NSA_Q_DIM = NSA_HEADS * HEAD_DIM
NSA_KV_DIM = 6 * NSA_KV_HEADS * HEAD_DIM

GDN_HEADS = 8
GDN_DK = 128
GDN_DV = 128
GDN_DIM = GDN_HEADS * GDN_DK
CONV_WIDTH = 4
GDN_CHUNK = 64

S5_GROUP = 16
S5_GROUPS = D_MODEL // S5_GROUP
S5_STATE = 64

D_FF = 2816
N_EXPERTS = 8
TOP_K = 2
D_FF_EXPERT = 3584

RMS_EPS = 1e-6
NEG_INF = -1e30

LANE = 128
VMEM_LIMIT_BYTES = 56 * 1024 * 1024


def _cparams(*sem):
    return pltpu.CompilerParams(dimension_semantics=sem, vmem_limit_bytes=VMEM_LIMIT_BYTES)


def _round_up(n, m):
    return -(-n // m) * m


def _norm_mod(x, nw, sc, sh):
    y = x * lax.rsqrt(jnp.mean(x * x, axis=-1, keepdims=True) + RMS_EPS)
    return (y * nw) * (1.0 + sc) + sh


def _mod_spec(mod, tm, rows_per_batch):
    if mod.shape[1] == 1:
        return pl.BlockSpec((1, 1, mod.shape[2]), lambda i, *_: (i * tm // rows_per_batch, 0, 0))
    return pl.BlockSpec((1, tm, mod.shape[2]), lambda i, *_: (0, i, 0))


def _adaln_kernel(c_ref, w_ref, b_ref, o_ref):
    c = c_ref[...]
    s = c / (1.0 + jnp.exp(-c))
    o_ref[0] = jnp.dot(s, w_ref[0], preferred_element_type=F32, precision=lax.Precision.HIGHEST) + b_ref[0]


def adaln_all(c, ada_w, ada_b):
    r, d = c.shape
    n = ada_w.shape[2]
    tn = 1536
    return pl.pallas_call(
        _adaln_kernel,
        grid=(DEPTH, n // tn),
        in_specs=[pl.BlockSpec((r, d), lambda l, j: (0, 0)),
                  pl.BlockSpec((1, d, tn), lambda l, j: (l, 0, j)),
                  pl.BlockSpec((1, 1, tn), lambda l, j: (l, 0, j))],
        out_specs=pl.BlockSpec((1, r, tn), lambda l, j: (l, 0, j)),
        out_shape=jax.ShapeDtypeStruct((DEPTH, r, n), F32),
        compiler_params=_cparams("arbitrary", "arbitrary"),
        name="adaln",
    )(c, ada_w, ada_b.reshape(DEPTH, 1, n))


def _dot(a, b):
    prec = lax.Precision.HIGHEST if b.dtype == F32 else None
    return jnp.dot(a.astype(b.dtype), b, preferred_element_type=F32, precision=prec)


def _ln_mm_kernel(x_ref, nw_ref, sc_ref, sh_ref, w_ref, o_ref, h_ref):
    @pl.when(pl.program_id(1) == 0)
    def _():
        h_ref[...] = _norm_mod(x_ref[...], nw_ref[...], sc_ref[0], sh_ref[0]).astype(h_ref.dtype)

    o_ref[...] = _dot(h_ref[...], w_ref[...])


def ln_matmul(x, nw, sc, sh, w, *, tm, tn, rows_per_batch):
    m, d = x.shape
    n = w.shape[1]
    return pl.pallas_call(
        _ln_mm_kernel,
        grid=(m // tm, n // tn),
        in_specs=[pl.BlockSpec((tm, d), lambda i, j: (i, 0)),
                  pl.BlockSpec((1, d), lambda i, j: (0, 0)),
                  _mod_spec(sc, tm, rows_per_batch),
                  _mod_spec(sh, tm, rows_per_batch),
                  pl.BlockSpec((d, tn), lambda i, j: (0, j))],
        out_specs=pl.BlockSpec((tm, tn), lambda i, j: (i, j)),
        out_shape=jax.ShapeDtypeStruct((m, n), F32),
        scratch_shapes=[pltpu.VMEM((tm, d), w.dtype)],
        compiler_params=_cparams("arbitrary", "arbitrary"),
        name="ln_matmul",
    )(x, nw.reshape(1, d), sc, sh, w)


def _mm_res_kernel(a_ref, w_ref, x_ref, g_ref, o_ref):
    o_ref[...] = x_ref[...] + g_ref[0] * _dot(a_ref[...], w_ref[...])


def matmul_residual(a, w, x, g, *, tm, rows_per_batch):
    m, k = a.shape
    n = w.shape[1]
    return pl.pallas_call(
        _mm_res_kernel,
        grid=(m // tm,),
        in_specs=[pl.BlockSpec((tm, k), lambda i: (i, 0)),
                  pl.BlockSpec((k, n), lambda i: (0, 0)),
                  pl.BlockSpec((tm, n), lambda i: (i, 0)),
                  _mod_spec(g, tm, rows_per_batch)],
        out_specs=pl.BlockSpec((tm, n), lambda i: (i, 0)),
        out_shape=jax.ShapeDtypeStruct((m, n), F32),
        compiler_params=_cparams("arbitrary"),
        name="matmul_residual",
    )(a, w, x, g)


def _silu(a):
    return a / (1.0 + jnp.exp(-a))


def _ffn_kernel(x_ref, nw_ref, sc_ref, sh_ref, g_ref, wa_ref, wb_ref, wd_ref, o_ref, h_ref, acc_ref):
    j = pl.program_id(1)

    @pl.when(j == 0)
    def _():
        h_ref[...] = _norm_mod(x_ref[...], nw_ref[...], sc_ref[0], sh_ref[0]).astype(h_ref.dtype)
        acc_ref[...] = jnp.zeros_like(acc_ref)

    h = h_ref[...]
    a = _dot(h, wa_ref[...])
    b = _dot(h, wb_ref[...])
    acc_ref[...] += _dot(_silu(a) * b, wd_ref[...])

    @pl.when(j == pl.num_programs(1) - 1)
    def _():
        o_ref[...] = x_ref[...] + g_ref[0] * acc_ref[...]


def dense_ffn(x, nw, sc, sh, g, w_gu, w_down, *, tm, tf, rows_per_batch):
    m, d = x.shape
    ff = w_down.shape[0]
    nj = ff // tf
    return pl.pallas_call(
        _ffn_kernel,
        grid=(m // tm, nj),
        in_specs=[pl.BlockSpec((tm, d), lambda i, j: (i, 0)),
                  pl.BlockSpec((1, d), lambda i, j: (0, 0)),
                  _mod_spec(sc, tm, rows_per_batch),
                  _mod_spec(sh, tm, rows_per_batch),
                  _mod_spec(g, tm, rows_per_batch),
                  pl.BlockSpec((d, tf), lambda i, j: (0, j)),
                  pl.BlockSpec((d, tf), lambda i, j: (0, nj + j)),
                  pl.BlockSpec((tf, d), lambda i, j: (j, 0))],
        out_specs=pl.BlockSpec((tm, d), lambda i, j: (i, 0)),
        out_shape=jax.ShapeDtypeStruct((m, d), F32),
        scratch_shapes=[pltpu.VMEM((tm, d), w_gu.dtype), pltpu.VMEM((tm, d), F32)],
        compiler_params=_cparams("arbitrary", "arbitrary"),
        name="dense_ffn",
    )(x, nw.reshape(1, d), sc, sh, g, w_gu, w_gu, w_down)


def _ln_router_kernel(x_ref, nw_ref, sc_ref, sh_ref, wr_ref, h_ref, lg_ref):
    h = _norm_mod(x_ref[...], nw_ref[...], sc_ref[0], sh_ref[0])
    h_ref[...] = h.astype(h_ref.dtype)
    lg_ref[...] = jnp.dot(h, wr_ref[...], preferred_element_type=F32, precision=lax.Precision.HIGHEST)


def ln_router(x, nw, sc, sh, w_router, *, tm, rows_per_batch, h_dtype):
    m, d = x.shape
    wr = jnp.pad(w_router, ((0, 0), (0, LANE - N_EXPERTS)))
    h, lg = pl.pallas_call(
        _ln_router_kernel,
        grid=(m // tm,),
        in_specs=[pl.BlockSpec((tm, d), lambda i: (i, 0)),
                  pl.BlockSpec((1, d), lambda i: (0, 0)),
                  _mod_spec(sc, tm, rows_per_batch),
                  _mod_spec(sh, tm, rows_per_batch),
                  pl.BlockSpec((d, LANE), lambda i: (0, 0))],
        out_specs=[pl.BlockSpec((tm, d), lambda i: (i, 0)),
                   pl.BlockSpec((tm, LANE), lambda i: (i, 0))],
        out_shape=[jax.ShapeDtypeStruct((m, d), h_dtype), jax.ShapeDtypeStruct((m, LANE), F32)],
        compiler_params=_cparams("arbitrary"),
        name="ln_router",
    )(x, nw.reshape(1, d), sc, sh, wr)
    return h, lg[:, :N_EXPERTS]


def _moe_kernel(be_ref, nu_ref, r_ref, wa_ref, wb_ref, wd_ref, o_ref, acc_ref):
    i = pl.program_id(0)
    j = pl.program_id(1)

    @pl.when(i < nu_ref[0])
    def _():
        @pl.when(j == 0)
        def _():
            acc_ref[...] = jnp.zeros_like(acc_ref)

        r = r_ref[...]
        a = _dot(r, wa_ref[0])
        b = _dot(r, wb_ref[0])
        acc_ref[...] += _dot(_silu(a) * b, wd_ref[0])

        @pl.when(j == pl.num_programs(1) - 1)
        def _():
            o_ref[...] = acc_ref[...]


def moe_experts(rows, block_e, n_used, w_gu, w_down, *, tm, tf):
    m, d = rows.shape
    ff = w_down.shape[1]
    nj = ff // tf
    n_blocks = m // tm

    def blk(i, nu):
        return jnp.minimum(i, nu[0] - 1)

    def jj(i, j, nu):
        return jnp.where(i < nu[0], j, nj - 1)

    grid_spec = pltpu.PrefetchScalarGridSpec(
        num_scalar_prefetch=2,
        grid=(n_blocks, nj),
        in_specs=[pl.BlockSpec((tm, d), lambda i, j, be, nu: (blk(i, nu), 0)),
                  pl.BlockSpec((1, d, tf), lambda i, j, be, nu: (be[blk(i, nu)], 0, jj(i, j, nu))),
                  pl.BlockSpec((1, d, tf), lambda i, j, be, nu: (be[blk(i, nu)], 0, nj + jj(i, j, nu))),
                  pl.BlockSpec((1, tf, d), lambda i, j, be, nu: (be[blk(i, nu)], jj(i, j, nu), 0))],
        out_specs=pl.BlockSpec((tm, d), lambda i, j, be, nu: (blk(i, nu), 0)),
        scratch_shapes=[pltpu.VMEM((tm, d), F32)],
    )
    return pl.pallas_call(
        _moe_kernel,
        grid_spec=grid_spec,
        out_shape=jax.ShapeDtypeStruct((m, d), F32),
        compiler_params=_cparams("arbitrary", "arbitrary"),
        name="moe_experts",
    )(block_e, n_used, rows, w_gu, w_gu, w_down)


def moe_layer(x, nw, sc, sh, g, w_router, w_gu, w_down, *, tm_ln, tm, tf, rows_per_batch):
    n_tok, d = x.shape
    n_rows = n_tok * TOP_K
    h, logits = ln_router(x, nw, sc, sh, w_router, tm=tm_ln, rows_per_batch=rows_per_batch, h_dtype=w_gu.dtype)
    top_logit, top_e = lax.top_k(logits, TOP_K)
    gate = jax.nn.softmax(top_logit, axis=-1)
    flat_e = top_e.reshape(-1)
    order = jnp.argsort(flat_e)
    e_sorted = flat_e[order]
    tok_sorted = order // TOP_K
    counts = jnp.bincount(flat_e, length=N_EXPERTS)
    padded = (counts + tm - 1) // tm * tm
    pad_end = jnp.cumsum(padded)
    pad_start = pad_end - padded
    grp_start = jnp.cumsum(counts) - counts
    dest = pad_start[e_sorted] + jnp.arange(n_rows) - grp_start[e_sorted]
    n_blocks = -(-n_rows // tm) + N_EXPERTS
    rows = jnp.zeros((n_blocks * tm, d), h.dtype).at[dest].set(h[tok_sorted])
    block_e = jnp.minimum(jnp.searchsorted(pad_end, jnp.arange(n_blocks) * tm, side='right'),
                          N_EXPERTS - 1).astype(jnp.int32)
    n_used = (pad_end[-1] // tm).astype(jnp.int32).reshape(1)
    out = moe_experts(rows, block_e, n_used, w_gu, w_down, tm=tm, tf=tf)
    pos = jnp.zeros((n_rows,), jnp.int32).at[order].set(dest.astype(jnp.int32)).reshape(n_tok, TOP_K)
    f = out[pos[:, 0]] * gate[:, 0:1] + out[pos[:, 1]] * gate[:, 1:2]
    if g.shape[1] == 1:
        gg = jnp.repeat(g[:, 0], rows_per_batch, axis=0)
    else:
        gg = g[0]
    return x + gg * f


def alibi_slopes():
    return jnp.exp2(-8.0 * (jnp.arange(NSA_HEADS, dtype=F32) + 1.0) / NSA_HEADS)


def nsa_split(proj, B, T):
    q = proj[..., :NSA_Q_DIM].reshape(B, T, NSA_HEADS, HEAD_DIM)
    kv = proj[..., NSA_Q_DIM:NSA_Q_DIM + NSA_KV_DIM].reshape(B, T, 6, NSA_KV_HEADS, HEAD_DIM)
    gates = jax.nn.sigmoid(proj[..., NSA_Q_DIM + NSA_KV_DIM:NSA_Q_DIM + NSA_KV_DIM + 3 * NSA_HEADS]
                           ).reshape(B, T, NSA_HEADS, 3)
    return q, kv, gates


def nsa_compress(seq, pe, w1, w2):
    B, L, G, dh = seq.shape
    r = CMP_BLOCK // CMP_STRIDE
    n_chunk = L // CMP_STRIDE
    nc = n_chunk - r + 1
    ch = seq.reshape(B, n_chunk, CMP_STRIDE, G, dh)
    blocks = jnp.concatenate([ch[:, j:j + nc] for j in range(r)], axis=2)
    blocks = blocks + pe[None, None, :, None, :]
    flat = blocks.transpose(0, 1, 3, 2, 4).reshape(B, nc, G, CMP_BLOCK * dh)
    return jax.nn.gelu(flat @ w1) @ w2


def cmp_to_sel_map(nc, nsb):
    c_start = jnp.arange(nc) * CMP_STRIDE
    s_start = jnp.arange(nsb) * SEL_BLOCK
    hit = (c_start[:, None] < s_start[None, :] + SEL_BLOCK) & (c_start[:, None] + CMP_BLOCK > s_start[None, :])
    return hit.astype(F32)


def nsa_attend_block(q, gates, q_pos, kc, vc, kc_pos, ks_t, vs_t, kw, vw, kw_pos):
    B, Tq = q.shape[:2]
    nc = kc.shape[1]
    nsb = ks_t.shape[2]
    n_sel = min(SEL_TOPK, nsb)
    slopes = alibi_slopes().reshape(NSA_KV_HEADS, NSA_HPG)
    qg = q.reshape(B, Tq, NSA_KV_HEADS, NSA_HPG, HEAD_DIM) * (HEAD_DIM ** -0.5)

    d_c = q_pos[:, None] - kc_pos[None, :]
    ok_c = (d_c >= 0)[None, :, None, None, :]
    s_c = jnp.einsum('btghd,bngd->btghn', qg, kc).astype(F32)
    s_c = s_c - slopes[None, None, :, :, None] * jnp.abs(d_c).astype(F32)[None, :, None, None, :]
    s_c = jnp.where(ok_c, s_c, NEG_INF)
    p_c = jax.nn.softmax(s_c, axis=-1) * ok_c
    o_c = jnp.einsum('btghn,bngd->btghd', p_c.astype(vc.dtype), vc)

    imp = jnp.einsum('btghn,nj->btgj', p_c, cmp_to_sel_map(nc, nsb))
    cur = q_pos // SEL_BLOCK
    blk = jnp.arange(nsb)
    is_cur = (blk[None, :] == cur[:, None])[None, :, None, :]
    is_past = (blk[None, :] < cur[:, None])[None, :, None, :]
    imp = jnp.where(is_cur, jnp.inf, jnp.where(is_past, imp, -jnp.inf))
    _, idx = lax.top_k(imp, n_sel)
    bi = jnp.arange(B)[:, None, None, None]
    gi = jnp.arange(NSA_KV_HEADS)[None, None, :, None]
    k_sel = ks_t[bi, gi, idx]
    v_sel = vs_t[bi, gi, idx]
    pos_s = idx[..., None] * SEL_BLOCK + jnp.arange(SEL_BLOCK)
    d_s = q_pos[None, :, None, None, None] - pos_s
    ok_s = (d_s >= 0)[:, :, :, None]
    s_s = jnp.einsum('btghd,btgnkd->btghnk', qg, k_sel).astype(F32)
    s_s = s_s - slopes[None, None, :, :, None, None] * jnp.abs(d_s).astype(F32)[:, :, :, None]
    s_s = jnp.where(ok_s, s_s, NEG_INF).reshape(B, Tq, NSA_KV_HEADS, NSA_HPG, n_sel * SEL_BLOCK)
    p_s = jax.nn.softmax(s_s, axis=-1).reshape(B, Tq, NSA_KV_HEADS, NSA_HPG, n_sel, SEL_BLOCK)
    o_s = jnp.einsum('btghnk,btgnkd->btghd', p_s.astype(v_sel.dtype), v_sel)

    d_w = q_pos[:, None] - kw_pos[None, :]
    ok_w = ((d_w >= 0) & (d_w < WINDOW) & (kw_pos >= 0)[None, :])[None, :, None, None, :]
    s_w = jnp.einsum('btghd,blgd->btghl', qg, kw).astype(F32)
    s_w = s_w - slopes[None, None, :, :, None] * jnp.abs(d_w).astype(F32)[None, :, None, None, :]
    p_w = jax.nn.softmax(jnp.where(ok_w, s_w, NEG_INF), axis=-1)
    o_w = jnp.einsum('btghl,blgd->btghd', p_w.astype(vw.dtype), vw)

    g = gates.reshape(B, Tq, NSA_KV_HEADS, NSA_HPG, 3).astype(q.dtype)
    o = g[..., 0:1] * o_c + g[..., 1:2] * o_s + g[..., 2:3] * o_w
    return o.reshape(B, Tq, NSA_Q_DIM)


def nsa_prompt_core(proj, B, T, pe, w1, w2):
    q, kv, gates = nsa_split(proj, B, T)
    kc = nsa_compress(kv[:, :, 0], pe[0], w1[0], w2[0])
    vc = nsa_compress(kv[:, :, 1], pe[1], w1[1], w2[1])
    kc_pos = jnp.arange(kc.shape[1]) * CMP_STRIDE + CMP_BLOCK - 1
    nsb = T // SEL_BLOCK
    ks_t = kv[:, :, 2].reshape(B, nsb, SEL_BLOCK, NSA_KV_HEADS, HEAD_DIM).transpose(0, 3, 1, 2, 4)
    vs_t = kv[:, :, 3].reshape(B, nsb, SEL_BLOCK, NSA_KV_HEADS, HEAD_DIM).transpose(0, 3, 1, 2, 4)
    win_pad = jnp.pad(kv[:, :, 4:], ((0, 0), (WINDOW, 0), (0, 0), (0, 0), (0, 0)))

    def q_block(i):
        start = i * NSA_Q_BLOCK
        qb = lax.dynamic_slice_in_dim(q, start, NSA_Q_BLOCK, axis=1)
        gb = lax.dynamic_slice_in_dim(gates, start, NSA_Q_BLOCK, axis=1)
        wb = lax.dynamic_slice_in_dim(win_pad, start, WINDOW + NSA_Q_BLOCK, axis=1)
        q_pos = start + jnp.arange(NSA_Q_BLOCK)
        kw_pos = start - WINDOW + jnp.arange(WINDOW + NSA_Q_BLOCK)
        return nsa_attend_block(qb, gb, q_pos, kc, vc, kc_pos, ks_t, vs_t, wb[:, :, 0], wb[:, :, 1], kw_pos)

    o = lax.map(q_block, jnp.arange(T // NSA_Q_BLOCK))
    o = o.transpose(1, 0, 2, 3).reshape(B, T, NSA_Q_DIM)
    return o, kv[:, :, :4], kv[:, T - min(WINDOW, T):, 4:]


def nsa_sample_core(proj, B, T, cache_kv, cache_win, page_table, pe, w1, w2):
    past_len = page_table.shape[1] * cache_kv.shape[1]
    win_len = cache_win.shape[1]
    q, kv, gates = nsa_split(proj, B, T)
    past = cache_kv[page_table].reshape(B, past_len, 4, NSA_KV_HEADS, HEAD_DIM)
    full = jnp.concatenate([past, kv[:, :, :4]], axis=1)
    L = past_len + T
    Lp = -(-L // SEL_BLOCK) * SEL_BLOCK
    full = jnp.pad(full, ((0, 0), (0, Lp - L), (0, 0), (0, 0), (0, 0)))
    kc = nsa_compress(full[:, :, 0], pe[0], w1[0], w2[0])
    vc = nsa_compress(full[:, :, 1], pe[1], w1[1], w2[1])
    kc_pos = jnp.arange(kc.shape[1]) * CMP_STRIDE + CMP_BLOCK - 1
    nsb = Lp // SEL_BLOCK
    ks_t = full[:, :, 2].reshape(B, nsb, SEL_BLOCK, NSA_KV_HEADS, HEAD_DIM).transpose(0, 3, 1, 2, 4)
    vs_t = full[:, :, 3].reshape(B, nsb, SEL_BLOCK, NSA_KV_HEADS, HEAD_DIM).transpose(0, 3, 1, 2, 4)
    win = jnp.concatenate([cache_win, kv[:, :, 4:]], axis=1)
    kw_pos = past_len - win_len + jnp.arange(win_len + T)
    q_pos = past_len + jnp.arange(T)
    o = nsa_attend_block(q, gates, q_pos, kc, vc, kc_pos, ks_t, vs_t, win[:, :, 0], win[:, :, 1], kw_pos)
    return o, kv[:, :, :4], win[:, win.shape[1] - win_len:]


NSA_TQ = 128
NSA_TK = 512


def _qk(q, k):
    return lax.dot_general(q, k, (((1,), (1,)), ((), ())), preferred_element_type=F32)


def _nsa_attn_kernel(q_ref, kc_ref, vc_ref, ks_ref, vs_ref, kw_ref, vw_ref, gate_ref, slope_ref, map_ref,
                     e_ref, o_ref, *, q_start, t_keys, n_cmp, win_keys, win_len):
    g = pl.program_id(1)
    i = pl.program_id(2)
    TQ, TK, H = NSA_TQ, NSA_TK, NSA_HPG
    R = H * TQ
    cdt = q_ref.dtype
    nsb = t_keys // SEL_BLOCK
    start = q_start + i * TQ
    q = q_ref[0, 0].reshape(R, HEAD_DIM)
    slope_col = jnp.concatenate([jnp.broadcast_to(slope_ref[0, h][:, 0:1], (TQ, 1)) for h in range(H)], axis=0)
    qpos = start + lax.broadcasted_iota(jnp.int32, (TQ, 1), 0)
    qpos_r = jnp.concatenate([qpos] * H, axis=0)

    kcpos = lax.broadcasted_iota(jnp.int32, (1, n_cmp), 1) * CMP_STRIDE + (CMP_BLOCK - 1)
    ok_c = kcpos <= qpos_r
    s = _qk(q, kc_ref[0, 0]) + slope_col * (kcpos - start).astype(F32)
    s = jnp.where(ok_c, s, NEG_INF)
    e = jnp.exp(s - jnp.max(s, axis=-1, keepdims=True))
    pn = jnp.where(ok_c, e, 0.0) * (1.0 / jnp.sum(e, axis=-1, keepdims=True))
    o_c = jnp.dot(pn.astype(cdt), vc_ref[0, 0], preferred_element_type=F32)
    p_grp = pn[0:TQ]
    for h in range(1, H):
        p_grp = p_grp + pn[h * TQ:(h + 1) * TQ]
    imp = jnp.dot(p_grp, map_ref[...], preferred_element_type=F32, precision=lax.Precision.HIGHEST)

    blk = lax.broadcasted_iota(jnp.int32, (1, nsb), 1)
    blk_f = blk.astype(F32)
    cur = qpos // SEL_BLOCK
    work = jnp.where(blk < cur, imp, -1.0)
    sel = jnp.where(blk == cur, 1.0, 0.0)
    for _ in range(min(SEL_TOPK, nsb) - 1):
        mx = jnp.max(work, axis=-1, keepdims=True)
        first = jnp.min(jnp.where(work == mx, blk_f, float(nsb)), axis=-1, keepdims=True)
        pick = jnp.logical_and(blk_f == first, mx >= 0.0)
        sel = jnp.where(pick, 1.0, sel)
        work = jnp.where(pick, -1.0, work)
    sel_b = sel.astype(cdt)

    def sel_tile(j, carry, diagonal):
        m, l, acc = carry
        off = pl.multiple_of(j * TK, TK)
        kpos = off + lax.broadcasted_iota(jnp.int32, (1, TK), 1)
        mb = (jnp.dot(sel_b, e_ref[j], preferred_element_type=F32) - 1.0) * (-NEG_INF)
        if diagonal:
            mb = jnp.where(kpos <= qpos, mb, NEG_INF)
        s = _qk(q, ks_ref[0, 0, pl.ds(off, TK), :]) + slope_col * (kpos - start).astype(F32)
        s = s + jnp.concatenate([mb] * H, axis=0)
        m_new = jnp.maximum(m, jnp.max(s, axis=-1, keepdims=True))
        alpha = jnp.exp(m - m_new)
        p = jnp.exp(s - m_new)
        l = alpha * l + jnp.sum(p, axis=-1, keepdims=True)
        acc = alpha * acc + jnp.dot(p.astype(cdt), vs_ref[0, 0, pl.ds(off, TK), :], preferred_element_type=F32)
        return m_new, l, acc

    init = (jnp.full((R, 1), NEG_INF, F32), jnp.zeros((R, 1), F32), jnp.zeros((R, HEAD_DIM), F32))
    j_diag = start // TK
    carry = lax.fori_loop(0, j_diag, lambda j, c: sel_tile(j, c, False), init)
    _, l_s, acc_s = sel_tile(j_diag, carry, True)
    o_s = acc_s * (1.0 / l_s)

    ws = pl.multiple_of(jnp.maximum(start - win_len, 0), TQ)
    kwpos = ws + lax.broadcasted_iota(jnp.int32, (1, win_keys), 1)
    d_w = qpos_r - kwpos
    ok_w = jnp.logical_and(d_w >= 0, d_w < win_len)
    s = _qk(q, kw_ref[0, 0, pl.ds(ws, win_keys), :]) + slope_col * (kwpos - start).astype(F32)
    s = jnp.where(ok_w, s, NEG_INF)
    e = jnp.exp(s - jnp.max(s, axis=-1, keepdims=True))
    o_w = jnp.dot(e.astype(cdt), vw_ref[0, 0, pl.ds(ws, win_keys), :], preferred_element_type=F32)
    o_w = o_w * (1.0 / jnp.sum(e, axis=-1, keepdims=True))

    graw = gate_ref[...]
    gsel = jnp.where(g == 0, graw[:, 0:3 * H], graw[:, 3 * H:6 * H])
    sig = 1.0 / (1.0 + jnp.exp(-gsel))
    for h in range(H):
        rows = slice(h * TQ, (h + 1) * TQ)
        o_h = (sig[:, 3 * h:3 * h + 1] * o_c[rows] + sig[:, 3 * h + 1:3 * h + 2] * o_s[rows]
               + sig[:, 3 * h + 2:3 * h + 3] * o_w[rows])
        o_ref[0, 0, h] = o_h.astype(o_ref.dtype)


def nsa_prompt_attention(proj, kc, vc, B, T, cdt=BF16):
    assert NSA_KV_HEADS == 2 and T % NSA_TK == 0 and NSA_TK % NSA_TQ == 0
    G, H, dh = NSA_KV_HEADS, NSA_HPG, HEAD_DIM
    nq = T // NSA_TQ
    nc = kc.shape[1]
    n_cmp = _round_up(nc, LANE)
    nsb = T // SEL_BLOCK
    q5 = (proj[:, :NSA_Q_DIM] * (dh ** -0.5)).astype(cdt).reshape(B, T, G, H, dh).transpose(0, 2, 3, 1, 4)
    kv = proj[:, NSA_Q_DIM:NSA_Q_DIM + NSA_KV_DIM].astype(cdt).reshape(B, T, 6, G, dh).transpose(2, 0, 3, 1, 4)
    pad_c = ((0, 0), (0, 0), (0, n_cmp - nc), (0, 0))
    kc4 = jnp.pad(kc.astype(cdt).transpose(0, 2, 1, 3), pad_c)
    vc4 = jnp.pad(vc.astype(cdt).transpose(0, 2, 1, 3), pad_c)
    slopes = jnp.broadcast_to(alibi_slopes().reshape(G, H, 1, 1), (G, H, 1, LANE))
    cmap = jnp.pad(cmp_to_sel_map(nc, nsb), ((0, n_cmp - nc), (0, 0)))
    key_blk = jnp.arange(T, dtype=jnp.int32) // SEL_BLOCK
    expand = (jnp.arange(nsb, dtype=jnp.int32)[:, None] == key_blk[None, :]).astype(cdt)
    expand = expand.reshape(nsb, T // NSA_TK, NSA_TK).transpose(1, 0, 2)
    gate_col = (NSA_Q_DIM + NSA_KV_DIM) // LANE
    seq = pl.BlockSpec((1, 1, T, dh), lambda b, g, i: (b, g, 0, 0))
    cmp_spec = pl.BlockSpec((1, 1, n_cmp, dh), lambda b, g, i: (b, g, 0, 0))
    kern = functools.partial(_nsa_attn_kernel, q_start=0, t_keys=T, n_cmp=n_cmp,
                             win_keys=WINDOW + NSA_TQ, win_len=WINDOW)
    o5 = pl.pallas_call(
        kern,
        grid=(B, G, nq),
        in_specs=[pl.BlockSpec((1, 1, H, NSA_TQ, dh), lambda b, g, i: (b, g, 0, i, 0)),
                  cmp_spec, cmp_spec, seq, seq, seq, seq,
                  pl.BlockSpec((NSA_TQ, LANE), lambda b, g, i: (b * nq + i, gate_col)),
                  pl.BlockSpec((1, H, 1, LANE), lambda b, g, i: (g, 0, 0, 0)),
                  pl.BlockSpec((n_cmp, nsb), lambda b, g, i: (0, 0)),
                  pl.BlockSpec((T // NSA_TK, nsb, NSA_TK), lambda b, g, i: (0, 0, 0))],
        out_specs=pl.BlockSpec((1, 1, H, NSA_TQ, dh), lambda b, g, i: (b, g, 0, i, 0)),
        out_shape=jax.ShapeDtypeStruct((B, G, H, T, dh), cdt),
        compiler_params=_cparams("arbitrary", "arbitrary", "arbitrary"),
        name="nsa_attention",
    )(q5, kc4, vc4, kv[2], kv[3], kv[4], kv[5], proj, slopes, cmap, expand)
    return o5.transpose(0, 3, 1, 2, 4).reshape(B * T, G * H * dh)


def causal_dwconv(xp, w):
    return lax.conv_general_dilated(xp, w[:, None, :], window_strides=(1,), padding='VALID',
                                    dimension_numbers=('NWC', 'WIO', 'NWC'),
                                    feature_group_count=xp.shape[-1])


def l2norm(x):
    return x * lax.rsqrt(jnp.sum(x * x, axis=-1, keepdims=True) + 1e-6)


def gated_delta_chunked(q, k, v, g, beta, s0, chunk):
    B, T, H, DK = q.shape
    DV = v.shape[-1]
    n = T // chunk

    def blk(a):
        return jnp.moveaxis(a.reshape(B, n, chunk, H, *a.shape[3:]), 3, 2)

    q, k, v, g, beta = blk(q), blk(k), blk(v), blk(g), blk(beta)
    gc = jnp.cumsum(g, axis=-1)
    causal = jnp.tril(jnp.ones((chunk, chunk), dtype=bool))
    strict = jnp.tril(jnp.ones((chunk, chunk), dtype=bool), -1)
    diff = gc[..., :, None] - gc[..., None, :]
    decay = jnp.where(causal, jnp.exp(jnp.where(causal, diff, 0.0)), 0.0)
    kk = jnp.einsum('bnhik,bnhjk->bnhij', k, k)
    lower = jnp.where(strict, beta[..., :, None] * kk * decay, 0.0)
    a_mat = lower + jnp.eye(chunk, dtype=lower.dtype)
    rhs = jnp.concatenate([v * beta[..., None], k * (beta * jnp.exp(gc))[..., None]], axis=-1)
    sol = lax.linalg.triangular_solve(a_mat, rhs, left_side=True, lower=True, unit_diagonal=True)
    u, w = sol[..., :DV], sol[..., DV:]
    qk = jnp.einsum('bnhik,bnhjk->bnhij', q, k) * decay
    q_dec = q * jnp.exp(gc)[..., None]
    k_dec = k * jnp.exp(gc[..., -1:] - gc)[..., None]
    g_last = jnp.exp(gc[..., -1])

    def step(s, xs):
        u_c, w_c, qk_c, qd_c, kd_c, gl_c = xs
        v_new = u_c - jnp.einsum('bhck,bhkv->bhcv', w_c, s)
        o_c = jnp.einsum('bhck,bhkv->bhcv', qd_c, s) + jnp.einsum('bhij,bhjv->bhiv', qk_c, v_new)
        s = s * gl_c[..., None, None] + jnp.einsum('bhck,bhcv->bhkv', kd_c, v_new)
        return s, o_c

    xs = tuple(jnp.moveaxis(a, 1, 0) for a in (u, w, qk, q_dec, k_dec, g_last))
    s, o = lax.scan(step, s0, xs)
    o = jnp.moveaxis(jnp.moveaxis(o, 0, 1), 2, 3).reshape(B, T, H, DV)
    return o, s


def gdn_core(proj, B, T, conv_buf, s0, conv_w, a_log, dt_bias, norm_w, chunk):
    qkv_raw = proj[..., :3 * GDN_DIM]
    z = proj[..., 3 * GDN_DIM:4 * GDN_DIM].reshape(B, T, GDN_HEADS, GDN_DV)
    a = proj[..., 4 * GDN_DIM:4 * GDN_DIM + GDN_HEADS]
    b = proj[..., 4 * GDN_DIM + GDN_HEADS:4 * GDN_DIM + 2 * GDN_HEADS]
    xpad = jnp.concatenate([conv_buf, qkv_raw], axis=1)
    qkv = jax.nn.silu(causal_dwconv(xpad, conv_w))
    q, k, v = jnp.split(qkv, 3, axis=-1)
    q = l2norm(q.reshape(B, T, GDN_HEADS, GDN_DK)) * (GDN_DK ** -0.5)
    k = l2norm(k.reshape(B, T, GDN_HEADS, GDN_DK))
    v = v.reshape(B, T, GDN_HEADS, GDN_DV)
    beta = jax.nn.sigmoid(b)
    g = -jnp.exp(a_log) * jax.nn.softplus(a + dt_bias)
    o, s = gated_delta_chunked(q, k, v, g, beta, s0, chunk)
    of = o * lax.rsqrt(jnp.mean(o * o, axis=-1, keepdims=True) + RMS_EPS) * norm_w
    o = of * jax.nn.silu(z)
    return o.reshape(B, T, GDN_HEADS * GDN_DV), xpad[:, xpad.shape[1] - (CONV_WIDTH - 1):], s


def complex_linear_combine(e1, e2):
    a1r, a1i, b1r, b1i = e1
    a2r, a2i, b2r, b2i = e2
    return (a2r * a1r - a2i * a1i, a2r * a1i + a2i * a1r,
            a2r * b1r - a2i * b1i + b2r, a2r * b1i + a2i * b1r + b2i)


def s5_core(u, B, T, h0r, h0i, lam_re, lam_im, b_re, b_im, c_re, c_im, d_skip, log_dt, w_glu):
    ug = u.reshape(B, T, S5_GROUPS, S5_GROUP)
    dt = jnp.exp(log_dt)[:, None]
    lr, li = lam_re, lam_im
    mag = jnp.exp(lr * dt)
    ar, ai = mag * jnp.cos(li * dt), mag * jnp.sin(li * dt)
    den = lr * lr + li * li
    fr = ((ar - 1.0) * lr + ai * li) / den
    fi = (ai * lr - (ar - 1.0) * li) / den
    bbar_re = fr[..., None] * b_re - fi[..., None] * b_im
    bbar_im = fr[..., None] * b_im + fi[..., None] * b_re
    bu_re = jnp.einsum('gpc,btgc->btgp', bbar_re, ug)
    bu_im = jnp.einsum('gpc,btgc->btgp', bbar_im, ug)
    bu_re = bu_re.at[:, 0].add(ar * h0r - ai * h0i)
    bu_im = bu_im.at[:, 0].add(ar * h0i + ai * h0r)
    a_re = jnp.broadcast_to(ar, bu_re.shape)
    a_im = jnp.broadcast_to(ai, bu_im.shape)
    _, _, hr, hi = lax.associative_scan(complex_linear_combine, (a_re, a_im, bu_re, bu_im), axis=1)
    y = jnp.einsum('gcp,btgp->btgc', c_re, hr) - jnp.einsum('gcp,btgp->btgc', c_im, hi)
    y = y.reshape(B, T, D_MODEL) + d_skip * u.reshape(B, T, D_MODEL)
    z = jax.nn.gelu(y)
    z = z * jax.nn.sigmoid(z @ w_glu)
    return z, hr[:, -1], hi[:, -1]


S5_LANES = S5_GROUPS * S5_STATE
S5_SETS = D_MODEL // LANE
S5_SET_LANES = S5_LANES // S5_SETS


def _gelu_tanh(x):
    return x * (0.5 * (1.0 + jnp.tanh(math.sqrt(2.0 / math.pi) * (x + 0.044715 * (x * x * x)))))


def _s5_kernel(u_ref, x_ref, g_ref, h0r_ref, h0i_ref, ar_ref, ai_ref, bb_ref, cre_ref, cim_ref, d_ref,
               wglu_ref, wout_ref, o_ref, fr_ref, fi_ref, hr_ref, hi_ref, cr_ref, ci_ref, pwr_ref, pwi_ref,
               *, n_streams, n_steps, chain, chunk_lanes):
    S, L, CW = n_streams, n_steps, chunk_lanes
    i = pl.program_id(1)

    @pl.when(i == 0)
    def _():
        pr, pi = ar_ref[...], ai_ref[...]
        a_r, a_i = pr, pi
        pwr_ref[0:1, :] = pr
        pwi_ref[0:1, :] = pi
        for l in range(1, L):
            pr, pi = pr * a_r - pi * a_i, pr * a_i + pi * a_r
            pwr_ref[l:l + 1, :] = pr
            pwi_ref[l:l + 1, :] = pi
        if chain:
            cr_ref[...] = h0r_ref[0]
            ci_ref[...] = h0i_ref[0]

    nb = CW // LANE

    def load_rows(ref, l, ch):
        parts = [ref[ch * nb + q, pl.ds(l, S, stride=L), :] for q in range(nb)]
        return parts[0] if nb == 1 else jnp.concatenate(parts, axis=1)

    def store_rows(ref, l, ch, val):
        for q in range(nb):
            ref[ch * nb + q, pl.ds(l, S, stride=L), :] = val[:, q * LANE:(q + 1) * LANE]

    bps = S5_SET_LANES // LANE
    for s in range(S5_SETS):
        res = _dot(u_ref[:, s * LANE:(s + 1) * LANE], bb_ref[s])
        for q in range(bps):
            hr_ref[s * bps + q] = res[:, q * LANE:(q + 1) * LANE]
            hi_ref[s * bps + q] = res[:, S5_SET_LANES + q * LANE:S5_SET_LANES + (q + 1) * LANE]

    for ch in range(S5_LANES // CW):
        lanes = slice(ch * CW, (ch + 1) * CW)
        a_r = jnp.broadcast_to(ar_ref[:, lanes], (S, CW))
        a_i = jnp.broadcast_to(ai_ref[:, lanes], (S, CW))
        if chain:
            init = (jnp.zeros((S, CW), F32), jnp.zeros((S, CW), F32))
        else:
            init = (h0r_ref[0, :, lanes], h0i_ref[0, :, lanes])

        def scan_body(l, carry, ch=ch, a_r=a_r, a_i=a_i):
            h_r, h_i = carry
            n_r = a_r * h_r - a_i * h_i + load_rows(hr_ref, l, ch)
            n_i = a_r * h_i + a_i * h_r + load_rows(hi_ref, l, ch)
            store_rows(hr_ref, l, ch, n_r)
            store_rows(hi_ref, l, ch, n_i)
            return n_r, n_i

        e_r, e_i = lax.fori_loop(0, L, scan_body, init)
        if chain:
            al_r, al_i = pwr_ref[L - 1:L, lanes], pwi_ref[L - 1:L, lanes]
            s_r, s_i = cr_ref[:, lanes], ci_ref[:, lanes]
            before_r, before_i = [], []
            for c in range(S):
                before_r.append(s_r)
                before_i.append(s_i)
                s_r, s_i = (e_r[c:c + 1] + al_r * s_r - al_i * s_i, e_i[c:c + 1] + al_r * s_i + al_i * s_r)
            cr_ref[:, lanes] = s_r
            ci_ref[:, lanes] = s_i
            b_r = jnp.concatenate(before_r, axis=0)
            b_i = jnp.concatenate(before_i, axis=0)

            def fix_body(l, carry, ch=ch, lanes=lanes, b_r=b_r, b_i=b_i):
                p_r = pwr_ref[pl.ds(l, 1), lanes]
                p_i = pwi_ref[pl.ds(l, 1), lanes]
                store_rows(hr_ref, l, ch, load_rows(hr_ref, l, ch) + (p_r * b_r - p_i * b_i))
                store_rows(hi_ref, l, ch, load_rows(hi_ref, l, ch) + (p_r * b_i + p_i * b_r))
                return carry

            lax.fori_loop(0, L, fix_body, 0)
        else:
            fr_ref[0, :, lanes] = e_r
            fi_ref[0, :, lanes] = e_i

    if chain:
        fr_ref[0] = cr_ref[...]
        fi_ref[0] = ci_ref[...]

    ys = []
    for s in range(S5_SETS):
        h_r = jnp.concatenate([hr_ref[s * bps + q] for q in range(bps)], axis=1)
        h_i = jnp.concatenate([hi_ref[s * bps + q] for q in range(bps)], axis=1)
        ys.append(_dot(h_r, cre_ref[s]) - _dot(h_i, cim_ref[s]))
    y = jnp.concatenate(ys, axis=1) + d_ref[...] * u_ref[...]
    z = _gelu_tanh(y)
    gate = _dot(z, wglu_ref[...])
    z = z * (1.0 / (1.0 + jnp.exp(-gate)))
    o_ref[...] = x_ref[...] + g_ref[0] * _dot(z, wout_ref[...])


def s5_discretize(lam_re, lam_im, b_re, b_im, c_re, c_im, log_dt, wdtype):
    dt = jnp.exp(log_dt)[:, None]
    mag = jnp.exp(lam_re * dt)
    ar, ai = mag * jnp.cos(lam_im * dt), mag * jnp.sin(lam_im * dt)
    den = lam_re * lam_re + lam_im * lam_im
    fr = ((ar - 1.0) * lam_re + ai * lam_im) / den
    fi = (ai * lam_re - (ar - 1.0) * lam_im) / den
    bbar_re = fr[..., None] * b_re - fi[..., None] * b_im
    bbar_im = fr[..., None] * b_im + fi[..., None] * b_re
    gps = S5_GROUPS // S5_SETS
    eye = jnp.eye(gps, dtype=F32)

    def in_blocks(bbar):
        bb = bbar.reshape(S5_SETS, gps, S5_STATE, S5_GROUP)
        return jnp.einsum('ab,sapc->sacbp', eye, bb).reshape(S5_SETS, LANE, S5_SET_LANES)

    def out_blocks(c):
        cc = c.reshape(S5_SETS, gps, S5_GROUP, S5_STATE)
        return jnp.einsum('ab,sacp->sapbc', eye, cc).reshape(S5_SETS, S5_SET_LANES, LANE)

    bb = jnp.concatenate([in_blocks(bbar_re), in_blocks(bbar_im)], axis=-1).astype(wdtype)
    return (ar.reshape(1, S5_LANES), ai.reshape(1, S5_LANES), bb,
            out_blocks(c_re).astype(wdtype), out_blocks(c_im).astype(wdtype))


def s5_mixer(u, x, g, h0r, h0i, disc, d_skip, w_glu, w_out, *, n_streams, n_steps, chain, chunk_lanes,
             rows_per_batch):
    m, d = u.shape
    tm = n_streams * n_steps
    ar, ai, bb, cre, cim = disc
    nb = h0r.shape[0]
    tiles_per_batch = m // tm // nb
    s0 = h0r.shape[1]
    const2 = lambda b, i: (0, 0)
    const3 = lambda b, i: (0, 0, 0)
    row = lambda b, i: (b * tiles_per_batch + i, 0)
    kern = functools.partial(_s5_kernel, n_streams=n_streams, n_steps=n_steps, chain=chain,
                             chunk_lanes=chunk_lanes)
    return pl.pallas_call(
        kern,
        grid=(nb, tiles_per_batch),
        in_specs=[pl.BlockSpec((tm, d), row),
                  pl.BlockSpec((tm, d), row),
                  pl.BlockSpec((1, 1, d), lambda b, i: (b, 0, 0)) if g.shape[1] == 1 else
                  pl.BlockSpec((1, tm, d), lambda b, i: (0, b * tiles_per_batch + i, 0)),
                  pl.BlockSpec((1, s0, S5_LANES), lambda b, i: (b, 0, 0)),
                  pl.BlockSpec((1, s0, S5_LANES), lambda b, i: (b, 0, 0)),
                  pl.BlockSpec((1, S5_LANES), const2),
                  pl.BlockSpec((1, S5_LANES), const2),
                  pl.BlockSpec(bb.shape, const3),
                  pl.BlockSpec(cre.shape, const3),
                  pl.BlockSpec(cim.shape, const3),
                  pl.BlockSpec((1, d), const2),
                  pl.BlockSpec((d, d), const2),
                  pl.BlockSpec((d, d), const2)],
        out_specs=[pl.BlockSpec((tm, d), row),
                   pl.BlockSpec((1, s0, S5_LANES), lambda b, i: (b, 0, 0)),
                   pl.BlockSpec((1, s0, S5_LANES), lambda b, i: (b, 0, 0))],
        out_shape=[jax.ShapeDtypeStruct((m, d), F32),
                   jax.ShapeDtypeStruct(h0r.shape, F32),
                   jax.ShapeDtypeStruct(h0r.shape, F32)],
        scratch_shapes=[pltpu.VMEM((S5_LANES // LANE, tm, LANE), F32),
                        pltpu.VMEM((S5_LANES // LANE, tm, LANE), F32),
                        pltpu.VMEM((1, S5_LANES), F32), pltpu.VMEM((1, S5_LANES), F32),
                        pltpu.VMEM((n_steps, S5_LANES), F32), pltpu.VMEM((n_steps, S5_LANES), F32)],
        compiler_params=_cparams("arbitrary", "arbitrary"),
        name="s5_mixer",
    )(u, x, g, h0r, h0i, ar, ai, bb, cre, cim, d_skip.reshape(1, d), w_glu, w_out)


def _pad_cols(w, n):
    return jnp.pad(w, ((0, 0), (0, n - w.shape[1])))


def kernel(x_prompt, x_sample, cache_nsa_kv, cache_nsa_win, state_gdn_s, state_gdn_conv, state_s5_re, state_s5_im, page_table, c_prompt, c_sample, ada_w, ada_b, norm_mix, norm_ffn, norm_final, nsa_w_in, nsa_cmp_pe, nsa_cmp_w1, nsa_cmp_w2, nsa_w_out, gdn_w_in, gdn_conv_w, gdn_a_log, gdn_dt_bias, gdn_norm, gdn_w_out, s5_w_in, s5_lambda_re, s5_lambda_im, s5_b_re, s5_b_im, s5_c_re, s5_c_im, s5_d, s5_log_dt, s5_w_glu, s5_w_out, ffn_w_gu, ffn_w_down, moe_router, moe_w_gu, moe_w_down):
    Bp, Tp, d = x_prompt.shape
    Bs, Ts, _ = x_sample.shape
    Mp, Ms = Bp * Tp, Bs * Ts
    xp = x_prompt.reshape(Mp, d)
    xs = x_sample.reshape(Ms, d)

    c_all = jnp.concatenate([c_prompt, c_sample], axis=0)
    r_pad = _round_up(c_all.shape[0], 8)
    mods = adaln_all(jnp.pad(c_all, ((0, r_pad - c_all.shape[0]), (0, 0))), ada_w, ada_b)

    def mods_of(i):
        parts = jnp.split(mods[i], 6, axis=-1)
        mp = [p[:Bp].reshape(Bp, 1, d) for p in parts]
        ms = [jnp.repeat(p[Bp:Bp + Bs], Ts, axis=0).reshape(1, Ms, d) for p in parts]
        return mp, ms

    P = dict(rows_per_batch=Tp)
    S = dict(rows_per_batch=Ts)

    nsa_kv_p, nsa_kv_s, nsa_win_p, nsa_win_s = [], [], [], []
    gdn_s_p, gdn_s_s, gdn_conv_p, gdn_conv_s = [], [], [], []
    s5_re_p, s5_re_s, s5_im_p, s5_im_s = [], [], [], []

    HI = "highest"
    for i in range(DEPTH):
        (sh1_p, sc1_p, g1_p, sh2_p, sc2_p, g2_p), (sh1_s, sc1_s, g1_s, sh2_s, sc2_s, g2_s) = mods_of(i)
        j = i // N_MIXERS
        if i % N_MIXERS == 0:
            n_in = _round_up(nsa_w_in.shape[2], LANE)
            w_in = _pad_cols(nsa_w_in[j], n_in)
            w_out = nsa_w_out[j]
            proj_p = ln_matmul(xp, norm_mix[i], sc1_p, sh1_p, w_in.astype(BF16), tm=512, tn=n_in, **P)
            proj_s = ln_matmul(xs, norm_mix[i], sc1_s, sh1_s, w_in, tm=Ms, tn=n_in // 3, **S)
            kv6 = proj_p[:, NSA_Q_DIM:NSA_Q_DIM + NSA_KV_DIM].reshape(Bp, Tp, 6, NSA_KV_HEADS, HEAD_DIM)
            kc_p = nsa_compress(kv6[:, :, 0], nsa_cmp_pe[j, 0], nsa_cmp_w1[j, 0], nsa_cmp_w2[j, 0])
            vc_p = nsa_compress(kv6[:, :, 1], nsa_cmp_pe[j, 1], nsa_cmp_w1[j, 1], nsa_cmp_w2[j, 1])
            o_p = nsa_prompt_attention(proj_p, kc_p, vc_p, Bp, Tp)
            kv_p, win_p = kv6[:, :, :4], kv6[:, Tp - min(WINDOW, Tp):, 4:]
            with jax.default_matmul_precision(HI):
                o_s, kv_s, win_s = nsa_sample_core(proj_s.reshape(Bs, Ts, n_in), Bs, Ts, cache_nsa_kv[j],
                                                   cache_nsa_win[j], page_table,
                                                   nsa_cmp_pe[j], nsa_cmp_w1[j], nsa_cmp_w2[j])
            nsa_kv_p.append(kv_p); nsa_kv_s.append(kv_s)
            nsa_win_p.append(win_p); nsa_win_s.append(win_s)
        elif i % N_MIXERS == 1:
            n_in = _round_up(gdn_w_in.shape[2], LANE)
            w_in = _pad_cols(gdn_w_in[j], n_in)
            w_out = gdn_w_out[j]
            proj_p = ln_matmul(xp, norm_mix[i], sc1_p, sh1_p, w_in.astype(BF16), tm=256, tn=n_in, **P)
            proj_s = ln_matmul(xs, norm_mix[i], sc1_s, sh1_s, w_in, tm=Ms, tn=n_in // 3, **S)
            buf0 = jnp.zeros((Bp, CONV_WIDTH - 1, 3 * GDN_DIM), F32)
            s00 = jnp.zeros((Bp, GDN_HEADS, GDN_DK, GDN_DV), F32)
            o_p, cv_p, st_p = gdn_core(proj_p.reshape(Bp, Tp, n_in), Bp, Tp, buf0, s00, gdn_conv_w[j],
                                       gdn_a_log[j], gdn_dt_bias[j], gdn_norm[j], GDN_CHUNK)
            with jax.default_matmul_precision(HI):
                o_s, cv_s, st_s = gdn_core(proj_s.reshape(Bs, Ts, n_in), Bs, Ts, state_gdn_conv[j],
                                           state_gdn_s[j], gdn_conv_w[j], gdn_a_log[j], gdn_dt_bias[j],
                                           gdn_norm[j], Ts)
            gdn_s_p.append(st_p); gdn_s_s.append(st_s)
            gdn_conv_p.append(cv_p); gdn_conv_s.append(cv_s)
        else:
            w_in = s5_w_in[j]
            w_out = s5_w_out[j]
            u_p = ln_matmul(xp, norm_mix[i], sc1_p, sh1_p, w_in.astype(BF16), tm=512, tn=d, **P)
            u_s = ln_matmul(xs, norm_mix[i], sc1_s, sh1_s, w_in, tm=Ms, tn=d, **S)
            s5p = (s5_lambda_re[j], s5_lambda_im[j], s5_b_re[j], s5_b_im[j], s5_c_re[j], s5_c_im[j], s5_log_dt[j])
            h00 = jnp.zeros((Bp, 1, S5_LANES), F32)
            xp, re_p, im_p = s5_mixer(u_p, xp, g1_p, h00, h00, s5_discretize(*s5p, BF16), s5_d[j],
                                      s5_w_glu[j].astype(BF16), w_out.astype(BF16),
                                      n_streams=8, n_steps=32, chain=True, chunk_lanes=512, **P)
            xs, re_s, im_s = s5_mixer(u_s, xs, g1_s, state_s5_re[j].reshape(1, Bs, S5_LANES),
                                      state_s5_im[j].reshape(1, Bs, S5_LANES), s5_discretize(*s5p, F32),
                                      s5_d[j], s5_w_glu[j], w_out,
                                      n_streams=Bs, n_steps=Ts, chain=False, chunk_lanes=LANE, **S)
            st_shape = (-1, S5_GROUPS, S5_STATE)
            s5_re_p.append(re_p.reshape(st_shape)); s5_re_s.append(re_s.reshape(st_shape))
            s5_im_p.append(im_p.reshape(st_shape)); s5_im_s.append(im_s.reshape(st_shape))
        if i % N_MIXERS != 2:
            xp = matmul_residual(o_p.reshape(Mp, -1), w_out.astype(BF16), xp, g1_p, tm=512, **P)
            xs = matmul_residual(o_s.reshape(Ms, -1), w_out, xs, g1_s, tm=Ms, **S)

        f = i // 2
        if i % 2 == 0:
            w_gu, w_down = ffn_w_gu[f], ffn_w_down[f]
            xp = dense_ffn(xp, norm_ffn[i], sc2_p, sh2_p, g2_p, w_gu.astype(BF16), w_down.astype(BF16),
                           tm=1024, tf=256, **P)
            xs = dense_ffn(xs, norm_ffn[i], sc2_s, sh2_s, g2_s, w_gu, w_down, tm=Ms, tf=256, **S)
        else:
            w_gu, w_down = moe_w_gu[f], moe_w_down[f]
            w_gu_b, w_down_b = w_gu.astype(BF16), w_down.astype(BF16)
            xp = moe_layer(xp, norm_ffn[i], sc2_p, sh2_p, g2_p, moe_router[f], w_gu_b, w_down_b,
                           tm_ln=512, tm=512, tf=512, **P)
            last = i == DEPTH - 1
            xs = moe_layer(xs, norm_ffn[i], sc2_s, sh2_s, g2_s, moe_router[f],
                           w_gu_b if last else w_gu, w_down_b if last else w_down,
                           tm_ln=Ms, tm=128, tf=512, **S)

    def final_norm(x):
        return x * lax.rsqrt(jnp.mean(x * x, axis=-1, keepdims=True) + RMS_EPS) * norm_final

    y_prompt = final_norm(xp).reshape(Bp, Tp, d)
    y_sample = final_norm(xs).reshape(Bs, Ts, d)
    return (y_prompt, y_sample, jnp.stack(nsa_kv_p), jnp.stack(nsa_kv_s), jnp.stack(nsa_win_p),
            jnp.stack(nsa_win_s), jnp.stack(gdn_s_p), jnp.stack(gdn_s_s), jnp.stack(gdn_conv_p),
            jnp.stack(gdn_conv_s), jnp.stack(s5_re_p), jnp.stack(s5_re_s), jnp.stack(s5_im_p),
            jnp.stack(s5_im_s))
```

```python
import functools
import math

import jax
import jax.numpy as jnp
from jax import lax
from jax.experimental import pallas as pl
from jax.experimental.pallas import tpu as pltpu

F32 = jnp.float32
BF16 = jnp.bfloat16

D_MODEL = 1024
DEPTH = 4
PAGE_SIZE = 128
N_MIXERS = 3

NSA_HEADS = 16
HEAD_DIM = D_MODEL // NSA_HEADS
NSA_KV_HEADS = 2
NSA_HPG = NSA_HEADS // NSA_KV_HEADS
CMP_BLOCK = 32
CMP_STRIDE = 16
SEL_BLOCK = 64
SEL_TOPK = 16
WINDOW = 512
NSA_Q_BLOCK = 128
NSA_Q_DIM = NSA_HEADS * HEAD_DIM
NSA_KV_DIM = 6 * NSA_KV_HEADS * HEAD_DIM

GDN_HEADS = 8
GDN_DK = 128
GDN_DV = 128
GDN_DIM = GDN_HEADS * GDN_DK
CONV_WIDTH = 4
GDN_CHUNK = 64

S5_GROUP = 16
S5_GROUPS = D_MODEL // S5_GROUP
S5_STATE = 64

D_FF = 2816
N_EXPERTS = 8
TOP_K = 2
D_FF_EXPERT = 3584

RMS_EPS = 1e-6
NEG_INF = -1e30

LANE = 128
VMEM_LIMIT_BYTES = 56 * 1024 * 1024


def _cparams(*sem):
    return pltpu.CompilerParams(dimension_semantics=sem, vmem_limit_bytes=VMEM_LIMIT_BYTES)


def _round_up(n, m):
    return -(-n // m) * m


def _norm_mod(x, nw, sc, sh):
    y = x * lax.rsqrt(jnp.mean(x * x, axis=-1, keepdims=True) + RMS_EPS)
    return (y * nw) * (1.0 + sc) + sh


def _mod_spec(mod, tm, rows_per_batch):
    if mod.shape[1] == 1:
        return pl.BlockSpec((1, 1, mod.shape[2]), lambda i, *_: (i * tm // rows_per_batch, 0, 0))
    return pl.BlockSpec((1, tm, mod.shape[2]), lambda i, *_: (0, i, 0))


def _adaln_kernel(c_ref, w_ref, b_ref, o_ref):
    c = c_ref[...]
    s = c / (1.0 + jnp.exp(-c))
    o_ref[0] = jnp.dot(s, w_ref[0], preferred_element_type=F32, precision=lax.Precision.HIGHEST) + b_ref[0]


def adaln_all(c, ada_w, ada_b):
    r, d = c.shape
    n = ada_w.shape[2]
    tn = 1536
    return pl.pallas_call(
        _adaln_kernel,
        grid=(DEPTH, n // tn),
        in_specs=[pl.BlockSpec((r, d), lambda l, j: (0, 0)),
                  pl.BlockSpec((1, d, tn), lambda l, j: (l, 0, j)),
                  pl.BlockSpec((1, 1, tn), lambda l, j: (l, 0, j))],
        out_specs=pl.BlockSpec((1, r, tn), lambda l, j: (l, 0, j)),
        out_shape=jax.ShapeDtypeStruct((DEPTH, r, n), F32),
        compiler_params=_cparams("arbitrary", "arbitrary"),
        name="adaln",
    )(c, ada_w, ada_b.reshape(DEPTH, 1, n))


def _dot(a, b):
    prec = lax.Precision.HIGHEST if b.dtype == F32 else None
    return jnp.dot(a.astype(b.dtype), b, preferred_element_type=F32, precision=prec)


def _ln_mm_kernel(x_ref, nw_ref, sc_ref, sh_ref, w_ref, o_ref, h_ref):
    @pl.when(pl.program_id(1) == 0)
    def _():
        h_ref[...] = _norm_mod(x_ref[...], nw_ref[...], sc_ref[0], sh_ref[0]).astype(h_ref.dtype)

    o_ref[...] = _dot(h_ref[...], w_ref[...])


def ln_matmul(x, nw, sc, sh, w, *, tm, tn, rows_per_batch):
    m, d = x.shape
    n = w.shape[1]
    return pl.pallas_call(
        _ln_mm_kernel,
        grid=(m // tm, n // tn),
        in_specs=[pl.BlockSpec((tm, d), lambda i, j: (i, 0)),
                  pl.BlockSpec((1, d), lambda i, j: (0, 0)),
                  _mod_spec(sc, tm, rows_per_batch),
                  _mod_spec(sh, tm, rows_per_batch),
                  pl.BlockSpec((d, tn), lambda i, j: (0, j))],
        out_specs=pl.BlockSpec((tm, tn), lambda i, j: (i, j)),
        out_shape=jax.ShapeDtypeStruct((m, n), F32),
        scratch_shapes=[pltpu.VMEM((tm, d), w.dtype)],
        compiler_params=_cparams("arbitrary", "arbitrary"),
        name="ln_matmul",
    )(x, nw.reshape(1, d), sc, sh, w)


def _mm_res_kernel(a_ref, w_ref, x_ref, g_ref, o_ref):
    o_ref[...] = x_ref[...] + g_ref[0] * _dot(a_ref[...], w_ref[...])


def matmul_residual(a, w, x, g, *, tm, rows_per_batch):
    m, k = a.shape
    n = w.shape[1]
    return pl.pallas_call(
        _mm_res_kernel,
        grid=(m // tm,),
        in_specs=[pl.BlockSpec((tm, k), lambda i: (i, 0)),
                  pl.BlockSpec((k, n), lambda i: (0, 0)),
                  pl.BlockSpec((tm, n), lambda i: (i, 0)),
                  _mod_spec(g, tm, rows_per_batch)],
        out_specs=pl.BlockSpec((tm, n), lambda i: (i, 0)),
        out_shape=jax.ShapeDtypeStruct((m, n), F32),
        compiler_params=_cparams("arbitrary"),
        name="matmul_residual",
    )(a, w, x, g)


def _silu(a):
    return a / (1.0 + jnp.exp(-a))


def _ffn_kernel(x_ref, nw_ref, sc_ref, sh_ref, g_ref, wa_ref, wb_ref, wd_ref, o_ref, h_ref, acc_ref):
    j = pl.program_id(1)

    @pl.when(j == 0)
    def _():
        h_ref[...] = _norm_mod(x_ref[...], nw_ref[...], sc_ref[0], sh_ref[0]).astype(h_ref.dtype)
        acc_ref[...] = jnp.zeros_like(acc_ref)

    h = h_ref[...]
    a = _dot(h, wa_ref[...])
    b = _dot(h, wb_ref[...])
    acc_ref[...] += _dot(_silu(a) * b, wd_ref[...])

    @pl.when(j == pl.num_programs(1) - 1)
    def _():
        o_ref[...] = x_ref[...] + g_ref[0] * acc_ref[...]


def dense_ffn(x, nw, sc, sh, g, w_gu, w_down, *, tm, tf, rows_per_batch):
    m, d = x.shape
    ff = w_down.shape[0]
    nj = ff // tf
    return pl.pallas_call(
        _ffn_kernel,
        grid=(m // tm, nj),
        in_specs=[pl.BlockSpec((tm, d), lambda i, j: (i, 0)),
                  pl.BlockSpec((1, d), lambda i, j: (0, 0)),
                  _mod_spec(sc, tm, rows_per_batch),
                  _mod_spec(sh, tm, rows_per_batch),
                  _mod_spec(g, tm, rows_per_batch),
                  pl.BlockSpec((d, tf), lambda i, j: (0, j)),
                  pl.BlockSpec((d, tf), lambda i, j: (0, nj + j)),
                  pl.BlockSpec((tf, d), lambda i, j: (j, 0))],
        out_specs=pl.BlockSpec((tm, d), lambda i, j: (i, 0)),
        out_shape=jax.ShapeDtypeStruct((m, d), F32),
        scratch_shapes=[pltpu.VMEM((tm, d), w_gu.dtype), pltpu.VMEM((tm, d), F32)],
        compiler_params=_cparams("arbitrary", "arbitrary"),
        name="dense_ffn",
    )(x, nw.reshape(1, d), sc, sh, g, w_gu, w_gu, w_down)


def _ln_router_kernel(x_ref, nw_ref, sc_ref, sh_ref, wr_ref, h_ref, lg_ref):
    h = _norm_mod(x_ref[...], nw_ref[...], sc_ref[0], sh_ref[0])
    h_ref[...] = h.astype(h_ref.dtype)
    lg_ref[...] = jnp.dot(h, wr_ref[...], preferred_element_type=F32, precision=lax.Precision.HIGHEST)


def ln_router(x, nw, sc, sh, w_router, *, tm, rows_per_batch, h_dtype):
    m, d = x.shape
    wr = jnp.pad(w_router, ((0, 0), (0, LANE - N_EXPERTS)))
    h, lg = pl.pallas_call(
        _ln_router_kernel,
        grid=(m // tm,),
        in_specs=[pl.BlockSpec((tm, d), lambda i: (i, 0)),
                  pl.BlockSpec((1, d), lambda i: (0, 0)),
                  _mod_spec(sc, tm, rows_per_batch),
                  _mod_spec(sh, tm, rows_per_batch),
                  pl.BlockSpec((d, LANE), lambda i: (0, 0))],
        out_specs=[pl.BlockSpec((tm, d), lambda i: (i, 0)),
                   pl.BlockSpec((tm, LANE), lambda i: (i, 0))],
        out_shape=[jax.ShapeDtypeStruct((m, d), h_dtype), jax.ShapeDtypeStruct((m, LANE), F32)],
        compiler_params=_cparams("arbitrary"),
        name="ln_router",
    )(x, nw.reshape(1, d), sc, sh, wr)
    return h, lg[:, :N_EXPERTS]


def _moe_kernel(be_ref, nu_ref, r_ref, wa_ref, wb_ref, wd_ref, o_ref, acc_ref):
    i = pl.program_id(0)
    j = pl.program_id(1)

    @pl.when(i < nu_ref[0])
    def _():
        @pl.when(j == 0)
        def _():
            acc_ref[...] = jnp.zeros_like(acc_ref)

        r = r_ref[...]
        a = _dot(r, wa_ref[0])
        b = _dot(r, wb_ref[0])
        acc_ref[...] += _dot(_silu(a) * b, wd_ref[0])

        @pl.when(j == pl.num_programs(1) - 1)
        def _():
            o_ref[...] = acc_ref[...]


def moe_experts(rows, block_e, n_used, w_gu, w_down, *, tm, tf):
    m, d = rows.shape
    ff = w_down.shape[1]
    nj = ff // tf
    n_blocks = m // tm

    def blk(i, nu):
        return jnp.minimum(i, nu[0] - 1)

    def jj(i, j, nu):
        return jnp.where(i < nu[0], j, nj - 1)

    grid_spec = pltpu.PrefetchScalarGridSpec(
        num_scalar_prefetch=2,
        grid=(n_blocks, nj),
        in_specs=[pl.BlockSpec((tm, d), lambda i, j, be, nu: (blk(i, nu), 0)),
                  pl.BlockSpec((1, d, tf), lambda i, j, be, nu: (be[blk(i, nu)], 0, jj(i, j, nu))),
                  pl.BlockSpec((1, d, tf), lambda i, j, be, nu: (be[blk(i, nu)], 0, nj + jj(i, j, nu))),
                  pl.BlockSpec((1, tf, d), lambda i, j, be, nu: (be[blk(i, nu)], jj(i, j, nu), 0))],
        out_specs=pl.BlockSpec((tm, d), lambda i, j, be, nu: (blk(i, nu), 0)),
        scratch_shapes=[pltpu.VMEM((tm, d), F32)],
    )
    return pl.pallas_call(
        _moe_kernel,
        grid_spec=grid_spec,
        out_shape=jax.ShapeDtypeStruct((m, d), F32),
        compiler_params=_cparams("arbitrary", "arbitrary"),
        name="moe_experts",
    )(block_e, n_used, rows, w_gu, w_gu, w_down)


def moe_layer(x, nw, sc, sh, g, w_router, w_gu, w_down, *, tm_ln, tm, tf, rows_per_batch):
    n_tok, d = x.shape
    n_rows = n_tok * TOP_K
    h, logits = ln_router(x, nw, sc, sh, w_router, tm=tm_ln, rows_per_batch=rows_per_batch, h_dtype=w_gu.dtype)
    top_logit, top_e = lax.top_k(logits, TOP_K)
    gate = jax.nn.softmax(top_logit, axis=-1)
    flat_e = top_e.reshape(-1)
    order = jnp.argsort(flat_e)
    e_sorted = flat_e[order]
    tok_sorted = order // TOP_K
    counts = jnp.bincount(flat_e, length=N_EXPERTS)
    padded = (counts + tm - 1) // tm * tm
    pad_end = jnp.cumsum(padded)
    pad_start = pad_end - padded
    grp_start = jnp.cumsum(counts) - counts
    dest = pad_start[e_sorted] + jnp.arange(n_rows) - grp_start[e_sorted]
    n_blocks = -(-n_rows // tm) + N_EXPERTS
    rows = jnp.zeros((n_blocks * tm, d), h.dtype).at[dest].set(h[tok_sorted])
    block_e = jnp.minimum(jnp.searchsorted(pad_end, jnp.arange(n_blocks) * tm, side='right'),
                          N_EXPERTS - 1).astype(jnp.int32)
    n_used = (pad_end[-1] // tm).astype(jnp.int32).reshape(1)
    out = moe_experts(rows, block_e, n_used, w_gu, w_down, tm=tm, tf=tf)
    pos = jnp.zeros((n_rows,), jnp.int32).at[order].set(dest.astype(jnp.int32)).reshape(n_tok, TOP_K)
    f = out[pos[:, 0]] * gate[:, 0:1] + out[pos[:, 1]] * gate[:, 1:2]
    if g.shape[1] == 1:
        gg = jnp.repeat(g[:, 0], rows_per_batch, axis=0)
    else:
        gg = g[0]
    return x + gg * f


def alibi_slopes():
    return jnp.exp2(-8.0 * (jnp.arange(NSA_HEADS, dtype=F32) + 1.0) / NSA_HEADS)


def nsa_split(proj, B, T):
    q = proj[..., :NSA_Q_DIM].reshape(B, T, NSA_HEADS, HEAD_DIM)
    kv = proj[..., NSA_Q_DIM:NSA_Q_DIM + NSA_KV_DIM].reshape(B, T, 6, NSA_KV_HEADS, HEAD_DIM)
    gates = jax.nn.sigmoid(proj[..., NSA_Q_DIM + NSA_KV_DIM:NSA_Q_DIM + NSA_KV_DIM + 3 * NSA_HEADS]
                           ).reshape(B, T, NSA_HEADS, 3)
    return q, kv, gates


def nsa_compress(seq, pe, w1, w2):
    B, L, G, dh = seq.shape
    r = CMP_BLOCK // CMP_STRIDE
    n_chunk = L // CMP_STRIDE
    nc = n_chunk - r + 1
    ch = seq.reshape(B, n_chunk, CMP_STRIDE, G, dh)
    blocks = jnp.concatenate([ch[:, j:j + nc] for j in range(r)], axis=2)
    blocks = blocks + pe[None, None, :, None, :]
    flat = blocks.transpose(0, 1, 3, 2, 4).reshape(B, nc, G, CMP_BLOCK * dh)
    return jax.nn.gelu(flat @ w1) @ w2


def cmp_to_sel_map(nc, nsb):
    c_start = jnp.arange(nc) * CMP_STRIDE
    s_start = jnp.arange(nsb) * SEL_BLOCK
    hit = (c_start[:, None] < s_start[None, :] + SEL_BLOCK) & (c_start[:, None] + CMP_BLOCK > s_start[None, :])
    return hit.astype(F32)


def nsa_attend_block(q, gates, q_pos, kc, vc, kc_pos, ks_t, vs_t, kw, vw, kw_pos):
    B, Tq = q.shape[:2]
    nc = kc.shape[1]
    nsb = ks_t.shape[2]
    n_sel = min(SEL_TOPK, nsb)
    slopes = alibi_slopes().reshape(NSA_KV_HEADS, NSA_HPG)
    qg = q.reshape(B, Tq, NSA_KV_HEADS, NSA_HPG, HEAD_DIM) * (HEAD_DIM ** -0.5)

    d_c = q_pos[:, None] - kc_pos[None, :]
    ok_c = (d_c >= 0)[None, :, None, None, :]
    s_c = jnp.einsum('btghd,bngd->btghn', qg, kc).astype(F32)
    s_c = s_c - slopes[None, None, :, :, None] * jnp.abs(d_c).astype(F32)[None, :, None, None, :]
    s_c = jnp.where(ok_c, s_c, NEG_INF)
    p_c = jax.nn.softmax(s_c, axis=-1) * ok_c
    o_c = jnp.einsum('btghn,bngd->btghd', p_c.astype(vc.dtype), vc)

    imp = jnp.einsum('btghn,nj->btgj', p_c, cmp_to_sel_map(nc, nsb))
    cur = q_pos // SEL_BLOCK
    blk = jnp.arange(nsb)
    is_cur = (blk[None, :] == cur[:, None])[None, :, None, :]
    is_past = (blk[None, :] < cur[:, None])[None, :, None, :]
    imp = jnp.where(is_cur, jnp.inf, jnp.where(is_past, imp, -jnp.inf))
    _, idx = lax.top_k(imp, n_sel)
    bi = jnp.arange(B)[:, None, None, None]
    gi = jnp.arange(NSA_KV_HEADS)[None, None, :, None]
    k_sel = ks_t[bi, gi, idx]
    v_sel = vs_t[bi, gi, idx]
    pos_s = idx[..., None] * SEL_BLOCK + jnp.arange(SEL_BLOCK)
    d_s = q_pos[None, :, None, None, None] - pos_s
    ok_s = (d_s >= 0)[:, :, :, None]
    s_s = jnp.einsum('btghd,btgnkd->btghnk', qg, k_sel).astype(F32)
    s_s = s_s - slopes[None, None, :, :, None, None] * jnp.abs(d_s).astype(F32)[:, :, :, None]
    s_s = jnp.where(ok_s, s_s, NEG_INF).reshape(B, Tq, NSA_KV_HEADS, NSA_HPG, n_sel * SEL_BLOCK)
    p_s = jax.nn.softmax(s_s, axis=-1).reshape(B, Tq, NSA_KV_HEADS, NSA_HPG, n_sel, SEL_BLOCK)
    o_s = jnp.einsum('btghnk,btgnkd->btghd', p_s.astype(v_sel.dtype), v_sel)

    d_w = q_pos[:, None] - kw_pos[None, :]
    ok_w = ((d_w >= 0) & (d_w < WINDOW) & (kw_pos >= 0)[None, :])[None, :, None, None, :]
    s_w = jnp.einsum('btghd,blgd->btghl', qg, kw).astype(F32)
    s_w = s_w - slopes[None, None, :, :, None] * jnp.abs(d_w).astype(F32)[None, :, None, None, :]
    p_w = jax.nn.softmax(jnp.where(ok_w, s_w, NEG_INF), axis=-1)
    o_w = jnp.einsum('btghl,blgd->btghd', p_w.astype(vw.dtype), vw)

    g = gates.reshape(B, Tq, NSA_KV_HEADS, NSA_HPG, 3).astype(q.dtype)
    o = g[..., 0:1] * o_c + g[..., 1:2] * o_s + g[..., 2:3] * o_w
    return o.reshape(B, Tq, NSA_Q_DIM)


def nsa_prompt_core(proj, B, T, pe, w1, w2):
    q, kv, gates = nsa_split(proj, B, T)
    kc = nsa_compress(kv[:, :, 0], pe[0], w1[0], w2[0])
    vc = nsa_compress(kv[:, :, 1], pe[1], w1[1], w2[1])
    kc_pos = jnp.arange(kc.shape[1]) * CMP_STRIDE + CMP_BLOCK - 1
    nsb = T // SEL_BLOCK
    ks_t = kv[:, :, 2].reshape(B, nsb, SEL_BLOCK, NSA_KV_HEADS, HEAD_DIM).transpose(0, 3, 1, 2, 4)
    vs_t = kv[:, :, 3].reshape(B, nsb, SEL_BLOCK, NSA_KV_HEADS, HEAD_DIM).transpose(0, 3, 1, 2, 4)
    win_pad = jnp.pad(kv[:, :, 4:], ((0, 0), (WINDOW, 0), (0, 0), (0, 0), (0, 0)))

    def q_block(i):
        start = i * NSA_Q_BLOCK
        qb = lax.dynamic_slice_in_dim(q, start, NSA_Q_BLOCK, axis=1)
        gb = lax.dynamic_slice_in_dim(gates, start, NSA_Q_BLOCK, axis=1)
        wb = lax.dynamic_slice_in_dim(win_pad, start, WINDOW + NSA_Q_BLOCK, axis=1)
        q_pos = start + jnp.arange(NSA_Q_BLOCK)
        kw_pos = start - WINDOW + jnp.arange(WINDOW + NSA_Q_BLOCK)
        return nsa_attend_block(qb, gb, q_pos, kc, vc, kc_pos, ks_t, vs_t, wb[:, :, 0], wb[:, :, 1], kw_pos)

    o = lax.map(q_block, jnp.arange(T // NSA_Q_BLOCK))
    o = o.transpose(1, 0, 2, 3).reshape(B, T, NSA_Q_DIM)
    return o, kv[:, :, :4], kv[:, T - min(WINDOW, T):, 4:]


def nsa_sample_core(proj, B, T, cache_kv, cache_win, page_table, pe, w1, w2):
    past_len = page_table.shape[1] * cache_kv.shape[1]
    win_len = cache_win.shape[1]
    q, kv, gates = nsa_split(proj, B, T)
    past = cache_kv[page_table].reshape(B, past_len, 4, NSA_KV_HEADS, HEAD_DIM)
    full = jnp.concatenate([past, kv[:, :, :4]], axis=1)
    L = past_len + T
    Lp = -(-L // SEL_BLOCK) * SEL_BLOCK
    full = jnp.pad(full, ((0, 0), (0, Lp - L), (0, 0), (0, 0), (0, 0)))
    kc = nsa_compress(full[:, :, 0], pe[0], w1[0], w2[0])
    vc = nsa_compress(full[:, :, 1], pe[1], w1[1], w2[1])
    kc_pos = jnp.arange(kc.shape[1]) * CMP_STRIDE + CMP_BLOCK - 1
    nsb = Lp // SEL_BLOCK
    ks_t = full[:, :, 2].reshape(B, nsb, SEL_BLOCK, NSA_KV_HEADS, HEAD_DIM).transpose(0, 3, 1, 2, 4)
    vs_t = full[:, :, 3].reshape(B, nsb, SEL_BLOCK, NSA_KV_HEADS, HEAD_DIM).transpose(0, 3, 1, 2, 4)
    win = jnp.concatenate([cache_win, kv[:, :, 4:]], axis=1)
    kw_pos = past_len - win_len + jnp.arange(win_len + T)
    q_pos = past_len + jnp.arange(T)
    o = nsa_attend_block(q, gates, q_pos, kc, vc, kc_pos, ks_t, vs_t, win[:, :, 0], win[:, :, 1], kw_pos)
    return o, kv[:, :, :4], win[:, win.shape[1] - win_len:]


NSA_TQ = 128
NSA_TK = 512


def _split3(x):
    hi = x.astype(BF16)
    return hi, (x - hi.astype(F32)).astype(BF16)


def _dot3(a, b, dims):
    dn = (dims, ((), ()))
    if b.dtype == BF16:
        return lax.dot_general(a.astype(BF16), b, dn, preferred_element_type=F32)
    a_hi, a_lo = _split3(a.astype(F32))
    b_hi, b_lo = _split3(b)
    return (lax.dot_general(a_hi, b_hi, dn, preferred_element_type=F32)
            + (lax.dot_general(a_hi, b_lo, dn, preferred_element_type=F32)
               + lax.dot_general(a_lo, b_hi, dn, preferred_element_type=F32)))


def _qk(q, k):
    return _dot3(q, k, ((1,), (1,)))


def _pv(p, v):
    return _dot3(p, v, ((1,), (0,)))


def _nsa_attn_kernel(*refs, tq, q_start, n_past_tiles, n_cmp, nsb, win_keys, win_len, win_start, tail_start):
    if tail_start is None:
        (q_ref, kc_ref, vc_ref, ks_ref, vs_ref, kw_ref, vw_ref, gate_ref, slope_ref, map_ref, e_ref, o_ref) = refs
    else:
        (q_ref, kc_ref, vc_ref, ks_ref, vs_ref, kw_ref, vw_ref, gate_ref, slope_ref, map_ref, e_ref,
         kt_ref, vt_ref, o_ref) = refs
    g = pl.program_id(1)
    i = pl.program_id(2)
    TQ, TK, H = tq, NSA_TK, NSA_HPG
    R = H * TQ
    start = q_start + i * TQ
    q = q_ref[0, 0].reshape(R, HEAD_DIM)
    slope_col = jnp.concatenate([jnp.broadcast_to(slope_ref[0, h][:, 0:1], (TQ, 1)) for h in range(H)], axis=0)
    qpos = start + lax.broadcasted_iota(jnp.int32, (TQ, 1), 0)
    qpos_r = jnp.concatenate([qpos] * H, axis=0)

    kcpos = lax.broadcasted_iota(jnp.int32, (1, n_cmp), 1) * CMP_STRIDE + (CMP_BLOCK - 1)
    ok_c = kcpos <= qpos_r
    s = _qk(q, kc_ref[0, 0]) + slope_col * (kcpos - start).astype(F32)
    s = jnp.where(ok_c, s, NEG_INF)
    e = jnp.exp(s - jnp.max(s, axis=-1, keepdims=True))
    pn = jnp.where(ok_c, e, 0.0) * (1.0 / jnp.sum(e, axis=-1, keepdims=True))
    o_c = _pv(pn, vc_ref[0, 0])
    p_grp = pn[0:TQ]
    for h in range(1, H):
        p_grp = p_grp + pn[h * TQ:(h + 1) * TQ]
    imp = jnp.dot(p_grp, map_ref[...], preferred_element_type=F32, precision=lax.Precision.HIGHEST)

    blk = lax.broadcasted_iota(jnp.int32, (1, nsb), 1)
    blk_f = blk.astype(F32)
    cur = qpos // SEL_BLOCK
    work = jnp.where(blk < cur, imp, -1.0)
    sel = jnp.where(blk == cur, 1.0, 0.0)
    for _ in range(min(SEL_TOPK, nsb) - 1):
        mx = jnp.max(work, axis=-1, keepdims=True)
        first = jnp.min(jnp.where(work == mx, blk_f, float(nsb)), axis=-1, keepdims=True)
        pick = jnp.logical_and(blk_f == first, mx >= 0.0)
        sel = jnp.where(pick, 1.0, sel)
        work = jnp.where(pick, -1.0, work)
    sel_b = sel.astype(BF16)

    def online_step(carry, k, v, kpos, mb):
        m, l, acc = carry
        s = _qk(q, k) + slope_col * (kpos - start).astype(F32)
        s = s + jnp.concatenate([mb] * H, axis=0)
        m_new = jnp.maximum(m, jnp.max(s, axis=-1, keepdims=True))
        alpha = jnp.exp(m - m_new)
        p = jnp.exp(s - m_new)
        l = alpha * l + jnp.sum(p, axis=-1, keepdims=True)
        return m_new, l, alpha * acc + _pv(p, v)

    def sel_tile(j, carry, diagonal):
        off = pl.multiple_of(j * TK, TK)
        kpos = off + lax.broadcasted_iota(jnp.int32, (1, TK), 1)
        mb = (jnp.dot(sel_b, e_ref[j], preferred_element_type=F32) - 1.0) * (-NEG_INF)
        if diagonal:
            mb = jnp.where(kpos <= qpos, mb, NEG_INF)
        return online_step(carry, ks_ref[0, 0, pl.ds(off, TK), :], vs_ref[0, 0, pl.ds(off, TK), :], kpos, mb)

    init = (jnp.full((R, 1), NEG_INF, F32), jnp.zeros((R, 1), F32), jnp.zeros((R, HEAD_DIM), F32))
    if tail_start is None:
        j_diag = start // TK
        carry = lax.fori_loop(0, j_diag, lambda j, c: sel_tile(j, c, False), init)
        _, l_s, acc_s = sel_tile(j_diag, carry, True)
    else:
        carry = lax.fori_loop(0, n_past_tiles, lambda j, c: sel_tile(j, c, False), init)
        n_tail = kt_ref.shape[2]
        ktpos = tail_start + lax.broadcasted_iota(jnp.int32, (1, n_tail), 1)
        mb = jnp.where(ktpos <= qpos, 0.0, NEG_INF)
        _, l_s, acc_s = online_step(carry, kt_ref[0, 0], vt_ref[0, 0], ktpos, mb)
    o_s = acc_s * (1.0 / l_s)

    if tail_start is None:
        ws = pl.multiple_of(jnp.maximum(start - win_len, 0), TQ)
        kw, vw = kw_ref[0, 0, pl.ds(ws, win_keys), :], vw_ref[0, 0, pl.ds(ws, win_keys), :]
    else:
        ws = win_start
        kw, vw = kw_ref[0, 0], vw_ref[0, 0]
    kwpos = ws + lax.broadcasted_iota(jnp.int32, (1, win_keys), 1)
    d_w = qpos_r - kwpos
    ok_w = jnp.logical_and(d_w >= 0, d_w < win_len)
    s = _qk(q, kw) + slope_col * (kwpos - start).astype(F32)
    s = jnp.where(ok_w, s, NEG_INF)
    e = jnp.exp(s - jnp.max(s, axis=-1, keepdims=True))
    o_w = _pv(e, vw) * (1.0 / jnp.sum(e, axis=-1, keepdims=True))

    graw = gate_ref[0]
    gsel = jnp.where(g == 0, graw[:, 0:3 * H], graw[:, 3 * H:6 * H])
    sig = 1.0 / (1.0 + jnp.exp(-gsel))
    for h in range(H):
        rows = slice(h * TQ, (h + 1) * TQ)
        o_h = (sig[:, 3 * h:3 * h + 1] * o_c[rows] + sig[:, 3 * h + 1:3 * h + 2] * o_s[rows]
               + sig[:, 3 * h + 2:3 * h + 3] * o_w[rows])
        o_ref[0, 0, h] = o_h.astype(o_ref.dtype)


LOG2E = 1.4426950408889634


def _nsa_prompt_kernel(q_ref, kc_ref, vc_ref, ks_ref, vs_ref, kw_ref, vw_ref, gate_ref, map_ref, o_ref,
                       *, n_cmp, nsb, win_keys, win_len):
    g = pl.program_id(1)
    i = pl.program_id(2)
    TQ, TK, H, dh = NSA_TQ, NSA_TK, NSA_HPG, HEAD_DIM
    R = H * TQ
    start = i * TQ
    qa = q_ref[0, 0].reshape(R, LANE)
    qpos = start + lax.broadcasted_iota(jnp.int32, (TQ, 1), 0)
    qpos_r = jnp.concatenate([qpos] * H, axis=0)

    kcpos = lax.broadcasted_iota(jnp.int32, (1, n_cmp), 1) * CMP_STRIDE + (CMP_BLOCK - 1)
    ok_c = kcpos <= qpos_r
    s = jnp.where(ok_c, _qk(qa, kc_ref[0, 0]), NEG_INF)
    e = jnp.exp2(s - jnp.max(s, axis=-1, keepdims=True))
    pn = jnp.where(ok_c, e, 0.0) * (1.0 / jnp.sum(e, axis=-1, keepdims=True))
    o_c = _pv(pn, vc_ref[0, 0])
    p_grp = pn[0:TQ]
    for h in range(1, H):
        p_grp = p_grp + pn[h * TQ:(h + 1) * TQ]
    imp = jnp.dot(p_grp, map_ref[...], preferred_element_type=F32, precision=lax.Precision.HIGHEST)

    blk = lax.broadcasted_iota(jnp.int32, (1, nsb), 1)
    blk_f = blk.astype(F32)
    cur = qpos // SEL_BLOCK
    work = jnp.where(blk < cur, imp, -1.0)
    sel = jnp.where(blk == cur, 1.0, 0.0)
    for _ in range(min(SEL_TOPK, nsb) - 1):
        mx = jnp.max(work, axis=-1, keepdims=True)
        first = jnp.min(jnp.where(work == mx, blk_f, float(nsb)), axis=-1, keepdims=True)
        pick = jnp.logical_and(blk_f == first, mx >= 0.0)
        sel = jnp.where(pick, 1.0, sel)
        work = jnp.where(pick, -1.0, work)
    neg_sel = ((sel - 1.0) * (-NEG_INF)).astype(qa.dtype)
    q_full = jnp.concatenate([qa, jnp.concatenate([neg_sel] * H, axis=0)], axis=1)

    def sel_tile(j, carry, diagonal):
        m, acc = carry
        off = pl.multiple_of(j * TK, TK)
        s = _qk(q_full, ks_ref[0, 0, pl.ds(off, TK), :])
        if diagonal:
            kpos = off + lax.broadcasted_iota(jnp.int32, (1, TK), 1)
            s = jnp.where(kpos <= qpos_r, s, NEG_INF)
        m_new = jnp.maximum(m, jnp.max(s, axis=-1, keepdims=True))
        p = jnp.exp2(s - m_new)
        return m_new, jnp.exp2(m - m_new) * acc + _pv(p, vs_ref[0, 0, pl.ds(off, TK), :])

    init = (jnp.full((R, 1), NEG_INF, F32), jnp.zeros((R, LANE), F32))
    j_diag = start // TK
    carry = lax.fori_loop(0, j_diag, lambda j, c: sel_tile(j, c, False), init)
    _, acc_s = sel_tile(j_diag, carry, True)
    o_s = acc_s[:, 0:dh] * (1.0 / acc_s[:, dh:dh + 1])

    ws = pl.multiple_of(jnp.maximum(start - win_len, 0), TQ)
    kwpos = ws + lax.broadcasted_iota(jnp.int32, (1, win_keys), 1)
    d_w = qpos_r - kwpos
    ok_w = jnp.logical_and(d_w >= 0, d_w < win_len)
    s = jnp.where(ok_w, _qk(qa, kw_ref[0, 0, pl.ds(ws, win_keys), :]), NEG_INF)
    e = jnp.exp2(s - jnp.max(s, axis=-1, keepdims=True))
    acc_w = _pv(e, vw_ref[0, 0, pl.ds(ws, win_keys), :])
    o_w = acc_w[:, 0:dh] * (1.0 / acc_w[:, dh:dh + 1])

    graw = gate_ref[0]
    gsel = jnp.where(g == 0, graw[:, 0:3 * H], graw[:, 3 * H:6 * H])
    sig = 1.0 / (1.0 + jnp.exp(-gsel))
    for h in range(H):
        rows = slice(h * TQ, (h + 1) * TQ)
        o_h = (sig[:, 3 * h:3 * h + 1] * o_c[rows] + sig[:, 3 * h + 1:3 * h + 2] * o_s[rows]
               + sig[:, 3 * h + 2:3 * h + 3] * o_w[rows])
        o_ref[0, 0, h] = o_h.astype(o_ref.dtype)


def _nsa_cmp_kernel(x_ref, pe_ref, w1_ref, w2_ref, o_ref):
    x = x_ref[0, 0]
    n, half = x.shape
    a = _dot3(x + pe_ref[0, 0:1, :], w1_ref[0, 0:half, :], ((1,), (0,)))
    b = _dot3(x + pe_ref[0, 1:2, :], w1_ref[0, half:, :], ((1,), (0,)))
    pre = a + pltpu.roll(b, n - 1, 0)
    o_ref[0, 0] = _dot3(_gelu_tanh(pre), w2_ref[0], ((1,), (0,))).astype(o_ref.dtype)


def nsa_compress_blocks(x, pe, w1, w2, out_dtype):
    kv, n, n_chunk, flat = x.shape
    hid = w1.shape[2]
    dh = w2.shape[2]
    return pl.pallas_call(
        _nsa_cmp_kernel,
        grid=(kv, n),
        in_specs=[pl.BlockSpec((1, 1, n_chunk, flat), lambda c, i: (c, i, 0, 0)),
                  pl.BlockSpec((1, 2, flat), lambda c, i: (c, 0, 0)),
                  pl.BlockSpec((1, 2 * flat, hid), lambda c, i: (c, 0, 0)),
                  pl.BlockSpec((1, hid, dh), lambda c, i: (c, 0, 0))],
        out_specs=pl.BlockSpec((1, 1, n_chunk, dh), lambda c, i: (c, i, 0, 0)),
        out_shape=jax.ShapeDtypeStruct((kv, n, n_chunk, dh), out_dtype),
        compiler_params=_cparams("arbitrary", "arbitrary"),
        name="nsa_compress",
    )(x, pe.reshape(kv, 2, flat).astype(F32), w1.astype(x.dtype), w2.astype(x.dtype))


def _nsa_consts(n_cmp, nsb, n_tiles):
    nsb_pad = _round_up(nsb, LANE)
    slopes = jnp.broadcast_to(alibi_slopes().reshape(NSA_KV_HEADS, NSA_HPG, 1, 1), (NSA_KV_HEADS, NSA_HPG, 1, LANE))
    cmap = jnp.pad(cmp_to_sel_map(n_cmp, nsb), ((0, 0), (0, nsb_pad - nsb)))
    key_blk = jnp.arange(n_tiles * NSA_TK, dtype=jnp.int32) // SEL_BLOCK
    expand = (jnp.arange(nsb_pad, dtype=jnp.int32)[:, None] == key_blk[None, :]).astype(BF16)
    expand = expand.reshape(nsb_pad, n_tiles, NSA_TK).transpose(1, 0, 2)
    return slopes, cmap, expand, nsb_pad


def nsa_prompt_attention(proj, B, T, pe, w1, w2, cdt=BF16):
    assert NSA_KV_HEADS == 2 and T % NSA_TK == 0 and NSA_TK % NSA_TQ == 0 and NSA_TQ % SEL_BLOCK == 0
    G, H, dh = NSA_KV_HEADS, NSA_HPG, HEAD_DIM
    nq = T // NSA_TQ
    n_cmp = T // CMP_STRIDE
    nsb = T // SEL_BLOCK
    n_in = proj.shape[1]
    q5 = (proj[:, :NSA_Q_DIM] * (dh ** -0.5 * LOG2E)).astype(cdt).reshape(B, T, G, H, dh).transpose(0, 2, 3, 1, 4)
    kv = proj[:, NSA_Q_DIM:NSA_Q_DIM + NSA_KV_DIM].astype(cdt).reshape(B, T, 6, G, dh)
    xcmp = kv[:, :, 0:2].reshape(B, n_cmp, CMP_STRIDE, 2, G, dh).transpose(3, 0, 4, 1, 2, 5)
    kvc = nsa_compress_blocks(xcmp.reshape(2, B * G, n_cmp, CMP_STRIDE * dh), pe, w1, w2, cdt)
    kvc = kvc.reshape(2, B, G, n_cmp, dh)
    kvt = kv.transpose(2, 0, 3, 1, 4)
    _, cmap, _, nsb_pad = _nsa_consts(n_cmp, nsb, T // NSA_TK)
    assert nsb_pad == LANE

    sl = alibi_slopes() * LOG2E
    s1 = sl.astype(BF16)
    s2 = (sl - s1.astype(F32)).astype(BF16)
    s3 = (sl - s1.astype(F32) - s2.astype(F32)).astype(BF16)
    q_cols = jnp.stack([s1, s2, s3, s1, s2, s3], axis=-1).astype(cdt).reshape(1, G, H, 1, 6)

    def pos_cols(pos):
        hi = (pos // SEL_BLOCK * SEL_BLOCK).astype(cdt)
        lo = (pos % SEL_BLOCK).astype(cdt)
        return jnp.stack([hi, hi, hi, lo, lo, lo], axis=-1)

    def with_cols(x, cols, width):
        n = x.shape[2]
        parts = [x, jnp.broadcast_to(cols, (B, G, n, cols.shape[-1]))]
        parts.append(jnp.zeros((B, G, n, width - dh - cols.shape[-1]), cdt))
        return jnp.concatenate(parts, axis=-1)

    pos = jnp.arange(T, dtype=jnp.int32)
    kcols = pos_cols(pos)
    onehot = (pos[:, None] // SEL_BLOCK == jnp.arange(nsb_pad, dtype=jnp.int32)[None, :]).astype(cdt)
    q_aug = jnp.concatenate([q5, jnp.broadcast_to(q_cols, (B, G, H, T, 6)),
                             jnp.zeros((B, G, H, T, LANE - dh - 6), cdt)], axis=-1)
    ks_aug = jnp.concatenate([with_cols(kvt[2], kcols, LANE),
                              jnp.broadcast_to(onehot, (B, G, T, nsb_pad))], axis=-1)
    kw_aug = with_cols(kvt[4], kcols, LANE)
    kc_aug = with_cols(kvc[0], pos_cols(jnp.arange(n_cmp, dtype=jnp.int32) * CMP_STRIDE + (CMP_BLOCK - 1)), LANE)
    ones_col = jnp.ones((1, 1, 1, 1), cdt)
    vs_aug = with_cols(kvt[3], ones_col, LANE)
    vw_aug = with_cols(kvt[5], ones_col, LANE)

    gate_col = (NSA_Q_DIM + NSA_KV_DIM) // LANE
    seq = lambda w: pl.BlockSpec((1, 1, T, w), lambda b, g, i: (b, g, 0, 0))
    kern = functools.partial(_nsa_prompt_kernel, n_cmp=n_cmp, nsb=nsb_pad, win_keys=WINDOW + NSA_TQ, win_len=WINDOW)
    o5 = pl.pallas_call(
        kern,
        grid=(B, G, nq),
        in_specs=[pl.BlockSpec((1, 1, H, NSA_TQ, LANE), lambda b, g, i: (b, g, 0, i, 0)),
                  pl.BlockSpec((1, 1, n_cmp, LANE), lambda b, g, i: (b, g, 0, 0)),
                  pl.BlockSpec((1, 1, n_cmp, dh), lambda b, g, i: (b, g, 0, 0)),
                  seq(2 * LANE), seq(LANE), seq(LANE), seq(LANE),
                  pl.BlockSpec((1, NSA_TQ, LANE), lambda b, g, i: (b * nq + i, 0, gate_col)),
                  pl.BlockSpec(cmap.shape, lambda b, g, i: (0, 0))],
        out_specs=pl.BlockSpec((1, 1, H, NSA_TQ, dh), lambda b, g, i: (b, g, 0, i, 0)),
        out_shape=jax.ShapeDtypeStruct((B, G, H, T, dh), cdt),
        compiler_params=_cparams("arbitrary", "arbitrary", "arbitrary"),
        name="nsa_attention",
    )(q_aug, kc_aug, kvc[1], ks_aug, vs_aug, kw_aug, vw_aug, proj.reshape(B * nq, NSA_TQ, n_in), cmap)
    return o5.transpose(0, 3, 1, 2, 4).reshape(B * T, G * H * dh)


NSA_TQ_DECODE = 8


def nsa_sample_attention(proj, B, T, cache_kv, cache_win, page_table, pe, w1, w2):
    G, H, dh, TQ = NSA_KV_HEADS, NSA_HPG, HEAD_DIM, NSA_TQ_DECODE
    n_pages, page = page_table.shape[1], cache_kv.shape[1]
    past = n_pages * page
    win_len = cache_win.shape[1]
    last_q = past + T - 1
    n_cmp = past // CMP_STRIDE
    assert past % NSA_TK == 0 and T <= TQ and past % SEL_BLOCK == 0 and T <= SEL_BLOCK
    assert (last_q - (CMP_BLOCK - 1)) // CMP_STRIDE * CMP_STRIDE + CMP_BLOCK <= past
    assert win_len == WINDOW and past >= win_len
    nsb = past // SEL_BLOCK + 1
    n_in = proj.shape[1]
    cpp = page // CMP_STRIDE
    xcmp = cache_kv[page_table, :, 0:2].reshape(B, n_pages, cpp, CMP_STRIDE, 2, G, dh)
    xcmp = xcmp.transpose(4, 0, 5, 1, 2, 3, 6).reshape(2, B * G, n_cmp, CMP_STRIDE * dh)
    kvc = nsa_compress_blocks(xcmp, pe, w1, w2, F32).reshape(2, B, G, n_cmp, dh)
    ks = cache_kv[page_table, :, 2].transpose(0, 3, 1, 2, 4).reshape(B, G, past, dh)
    vs = cache_kv[page_table, :, 3].transpose(0, 3, 1, 2, 4).reshape(B, G, past, dh)
    kvn = proj[:, NSA_Q_DIM:NSA_Q_DIM + NSA_KV_DIM].reshape(B, T, 6, G, dh)

    def rows_pad(a, n):
        a = a.transpose(0, 2, 1, 3)
        return jnp.pad(a, ((0, 0), (0, 0), (0, n - a.shape[2]), (0, 0)))

    kt, vt = rows_pad(kvn[:, :, 2], SEL_BLOCK), rows_pad(kvn[:, :, 3], SEL_BLOCK)
    win = jnp.concatenate([cache_win, kvn[:, :, 4:]], axis=1)
    win_keys = _round_up(win_len + T, 8)
    kw, vw = rows_pad(win[:, :, 0], win_keys), rows_pad(win[:, :, 1], win_keys)
    q5 = (proj[:, :NSA_Q_DIM] * (dh ** -0.5)).reshape(B, T, G, H, dh).transpose(0, 2, 3, 1, 4)
    q5 = jnp.pad(q5, ((0, 0), (0, 0), (0, 0), (0, TQ - T), (0, 0)))
    gate_col = NSA_Q_DIM + NSA_KV_DIM
    gates = jnp.pad(proj[:, gate_col:gate_col + LANE].reshape(B, T, LANE), ((0, 0), (0, TQ - T), (0, 0)))
    slopes, cmap, expand, nsb_pad = _nsa_consts(n_cmp, nsb, past // NSA_TK)
    per_bg = lambda n: pl.BlockSpec((1, 1, n, dh), lambda b, g, i: (b, g, 0, 0))
    kern = functools.partial(_nsa_attn_kernel, tq=TQ, q_start=past, n_past_tiles=past // NSA_TK, n_cmp=n_cmp,
                             nsb=nsb_pad, win_keys=win_keys, win_len=win_len, win_start=past - win_len,
                             tail_start=past)
    o5 = pl.pallas_call(
        kern,
        grid=(B, G, 1),
        in_specs=[pl.BlockSpec((1, 1, H, TQ, dh), lambda b, g, i: (b, g, 0, 0, 0)),
                  per_bg(n_cmp), per_bg(n_cmp), per_bg(past), per_bg(past), per_bg(win_keys), per_bg(win_keys),
                  pl.BlockSpec((1, TQ, LANE), lambda b, g, i: (b, 0, 0)),
                  pl.BlockSpec((1, H, 1, LANE), lambda b, g, i: (g, 0, 0, 0)),
                  pl.BlockSpec(cmap.shape, lambda b, g, i: (0, 0)),
                  pl.BlockSpec(expand.shape, lambda b, g, i: (0, 0, 0)),
                  per_bg(SEL_BLOCK), per_bg(SEL_BLOCK)],
        out_specs=pl.BlockSpec((1, 1, H, TQ, dh), lambda b, g, i: (b, g, 0, 0, 0)),
        out_shape=jax.ShapeDtypeStruct((B, G, H, TQ, dh), F32),
        compiler_params=_cparams("arbitrary", "arbitrary", "arbitrary"),
        name="nsa_attention_decode",
    )(q5, kvc[0], kvc[1], ks, vs, kw, vw, gates, slopes, cmap, expand, kt, vt)
    o = o5[:, :, :, :T].transpose(0, 3, 1, 2, 4).reshape(B * T, G * H * dh)
    return o, win[:, win.shape[1] - win_len:]


def causal_dwconv(xp, w):
    return lax.conv_general_dilated(xp, w[:, None, :], window_strides=(1,), padding='VALID',
                                    dimension_numbers=('NWC', 'WIO', 'NWC'),
                                    feature_group_count=xp.shape[-1])


def l2norm(x):
    return x * lax.rsqrt(jnp.sum(x * x, axis=-1, keepdims=True) + 1e-6)


def gated_delta_chunked(q, k, v, g, beta, s0, chunk):
    B, T, H, DK = q.shape
    DV = v.shape[-1]
    n = T // chunk

    def blk(a):
        return jnp.moveaxis(a.reshape(B, n, chunk, H, *a.shape[3:]), 3, 2)

    q, k, v, g, beta = blk(q), blk(k), blk(v), blk(g), blk(beta)
    gc = jnp.cumsum(g, axis=-1)
    causal = jnp.tril(jnp.ones((chunk, chunk), dtype=bool))
    strict = jnp.tril(jnp.ones((chunk, chunk), dtype=bool), -1)
    diff = gc[..., :, None] - gc[..., None, :]
    decay = jnp.where(causal, jnp.exp(jnp.where(causal, diff, 0.0)), 0.0)
    kk = jnp.einsum('bnhik,bnhjk->bnhij', k, k)
    lower = jnp.where(strict, beta[..., :, None] * kk * decay, 0.0)
    a_mat = lower + jnp.eye(chunk, dtype=lower.dtype)
    rhs = jnp.concatenate([v * beta[..., None], k * (beta * jnp.exp(gc))[..., None]], axis=-1)
    sol = lax.linalg.triangular_solve(a_mat, rhs, left_side=True, lower=True, unit_diagonal=True)
    u, w = sol[..., :DV], sol[..., DV:]
    qk = jnp.einsum('bnhik,bnhjk->bnhij', q, k) * decay
    q_dec = q * jnp.exp(gc)[..., None]
    k_dec = k * jnp.exp(gc[..., -1:] - gc)[..., None]
    g_last = jnp.exp(gc[..., -1])

    def step(s, xs):
        u_c, w_c, qk_c, qd_c, kd_c, gl_c = xs
        v_new = u_c - jnp.einsum('bhck,bhkv->bhcv', w_c, s)
        o_c = jnp.einsum('bhck,bhkv->bhcv', qd_c, s) + jnp.einsum('bhij,bhjv->bhiv', qk_c, v_new)
        s = s * gl_c[..., None, None] + jnp.einsum('bhck,bhcv->bhkv', kd_c, v_new)
        return s, o_c

    xs = tuple(jnp.moveaxis(a, 1, 0) for a in (u, w, qk, q_dec, k_dec, g_last))
    s, o = lax.scan(step, s0, xs)
    o = jnp.moveaxis(jnp.moveaxis(o, 0, 1), 2, 3).reshape(B, T, H, DV)
    return o, s


def gdn_core(proj, B, T, conv_buf, s0, conv_w, a_log, dt_bias, norm_w, chunk):
    qkv_raw = proj[..., :3 * GDN_DIM]
    z = proj[..., 3 * GDN_DIM:4 * GDN_DIM].reshape(B, T, GDN_HEADS, GDN_DV)
    a = proj[..., 4 * GDN_DIM:4 * GDN_DIM + GDN_HEADS]
    b = proj[..., 4 * GDN_DIM + GDN_HEADS:4 * GDN_DIM + 2 * GDN_HEADS]
    xpad = jnp.concatenate([conv_buf, qkv_raw], axis=1)
    qkv = jax.nn.silu(causal_dwconv(xpad, conv_w))
    q, k, v = jnp.split(qkv, 3, axis=-1)
    q = l2norm(q.reshape(B, T, GDN_HEADS, GDN_DK)) * (GDN_DK ** -0.5)
    k = l2norm(k.reshape(B, T, GDN_HEADS, GDN_DK))
    v = v.reshape(B, T, GDN_HEADS, GDN_DV)
    beta = jax.nn.sigmoid(b)
    g = -jnp.exp(a_log) * jax.nn.softplus(a + dt_bias)
    o, s = gated_delta_chunked(q, k, v, g, beta, s0, chunk)
    of = o * lax.rsqrt(jnp.mean(o * o, axis=-1, keepdims=True) + RMS_EPS) * norm_w
    o = of * jax.nn.silu(z)
    return o.reshape(B, T, GDN_HEADS * GDN_DV), xpad[:, xpad.shape[1] - (CONV_WIDTH - 1):], s


def complex_linear_combine(e1, e2):
    a1r, a1i, b1r, b1i = e1
    a2r, a2i, b2r, b2i = e2
    return (a2r * a1r - a2i * a1i, a2r * a1i + a2i * a1r,
            a2r * b1r - a2i * b1i + b2r, a2r * b1i + a2i * b1r + b2i)


def s5_core(u, B, T, h0r, h0i, lam_re, lam_im, b_re, b_im, c_re, c_im, d_skip, log_dt, w_glu):
    ug = u.reshape(B, T, S5_GROUPS, S5_GROUP)
    dt = jnp.exp(log_dt)[:, None]
    lr, li = lam_re, lam_im
    mag = jnp.exp(lr * dt)
    ar, ai = mag * jnp.cos(li * dt), mag * jnp.sin(li * dt)
    den = lr * lr + li * li
    fr = ((ar - 1.0) * lr + ai * li) / den
    fi = (ai * lr - (ar - 1.0) * li) / den
    bbar_re = fr[..., None] * b_re - fi[..., None] * b_im
    bbar_im = fr[..., None] * b_im + fi[..., None] * b_re
    bu_re = jnp.einsum('gpc,btgc->btgp', bbar_re, ug)
    bu_im = jnp.einsum('gpc,btgc->btgp', bbar_im, ug)
    bu_re = bu_re.at[:, 0].add(ar * h0r - ai * h0i)
    bu_im = bu_im.at[:, 0].add(ar * h0i + ai * h0r)
    a_re = jnp.broadcast_to(ar, bu_re.shape)
    a_im = jnp.broadcast_to(ai, bu_im.shape)
    _, _, hr, hi = lax.associative_scan(complex_linear_combine, (a_re, a_im, bu_re, bu_im), axis=1)
    y = jnp.einsum('gcp,btgp->btgc', c_re, hr) - jnp.einsum('gcp,btgp->btgc', c_im, hi)
    y = y.reshape(B, T, D_MODEL) + d_skip * u.reshape(B, T, D_MODEL)
    z = jax.nn.gelu(y)
    z = z * jax.nn.sigmoid(z @ w_glu)
    return z, hr[:, -1], hi[:, -1]


S5_LANES = S5_GROUPS * S5_STATE
S5_SETS = D_MODEL // LANE
S5_SET_LANES = S5_LANES // S5_SETS


def _gelu_tanh(x):
    return x * (0.5 * (1.0 + jnp.tanh(math.sqrt(2.0 / math.pi) * (x + 0.044715 * (x * x * x)))))


def _s5_kernel(u_ref, x_ref, g_ref, h0r_ref, h0i_ref, ar_ref, ai_ref, bb_ref, cre_ref, cim_ref, d_ref,
               wglu_ref, wout_ref, o_ref, fr_ref, fi_ref, hr_ref, hi_ref, cr_ref, ci_ref, pwr_ref, pwi_ref,
               *, n_streams, n_steps, chain, chunk_lanes):
    S, L, CW = n_streams, n_steps, chunk_lanes
    i = pl.program_id(1)

    @pl.when(i == 0)
    def _():
        pr, pi = ar_ref[...], ai_ref[...]
        a_r, a_i = pr, pi
        pwr_ref[0:1, :] = pr
        pwi_ref[0:1, :] = pi
        for l in range(1, L):
            pr, pi = pr * a_r - pi * a_i, pr * a_i + pi * a_r
            pwr_ref[l:l + 1, :] = pr
            pwi_ref[l:l + 1, :] = pi
        if chain:
            cr_ref[...] = h0r_ref[0]
            ci_ref[...] = h0i_ref[0]

    nb = CW // LANE

    def load_rows(ref, l, ch):
        parts = [ref[ch * nb + q, pl.ds(l, S, stride=L), :] for q in range(nb)]
        return parts[0] if nb == 1 else jnp.concatenate(parts, axis=1)

    def store_rows(ref, l, ch, val):
        for q in range(nb):
            ref[ch * nb + q, pl.ds(l, S, stride=L), :] = val[:, q * LANE:(q + 1) * LANE]

    bps = S5_SET_LANES // LANE
    for s in range(S5_SETS):
        res = _dot(u_ref[:, s * LANE:(s + 1) * LANE], bb_ref[s])
        for q in range(bps):
            hr_ref[s * bps + q] = res[:, q * LANE:(q + 1) * LANE]
            hi_ref[s * bps + q] = res[:, S5_SET_LANES + q * LANE:S5_SET_LANES + (q + 1) * LANE]

    for ch in range(S5_LANES // CW):
        lanes = slice(ch * CW, (ch + 1) * CW)
        a_r = jnp.broadcast_to(ar_ref[:, lanes], (S, CW))
        a_i = jnp.broadcast_to(ai_ref[:, lanes], (S, CW))
        if chain:
            init = (jnp.zeros((S, CW), F32), jnp.zeros((S, CW), F32))
        else:
            init = (h0r_ref[0, :, lanes], h0i_ref[0, :, lanes])

        def scan_body(l, carry, ch=ch, a_r=a_r, a_i=a_i):
            h_r, h_i = carry
            n_r = a_r * h_r - a_i * h_i + load_rows(hr_ref, l, ch)
            n_i = a_r * h_i + a_i * h_r + load_rows(hi_ref, l, ch)
            store_rows(hr_ref, l, ch, n_r)
            store_rows(hi_ref, l, ch, n_i)
            return n_r, n_i

        e_r, e_i = lax.fori_loop(0, L, scan_body, init, unroll=min(L, 8))
        if chain:
            al_r, al_i = pwr_ref[L - 1:L, lanes], pwi_ref[L - 1:L, lanes]
            s_r, s_i = cr_ref[:, lanes], ci_ref[:, lanes]
            before_r, before_i = [], []
            for c in range(S):
                before_r.append(s_r)
                before_i.append(s_i)
                s_r, s_i = (e_r[c:c + 1] + al_r * s_r - al_i * s_i, e_i[c:c + 1] + al_r * s_i + al_i * s_r)
            cr_ref[:, lanes] = s_r
            ci_ref[:, lanes] = s_i
            b_r = jnp.concatenate(before_r, axis=0)
            b_i = jnp.concatenate(before_i, axis=0)

            def fix_body(l, carry, ch=ch, lanes=lanes, b_r=b_r, b_i=b_i):
                p_r = pwr_ref[pl.ds(l, 1), lanes]
                p_i = pwi_ref[pl.ds(l, 1), lanes]
                store_rows(hr_ref, l, ch, load_rows(hr_ref, l, ch) + (p_r * b_r - p_i * b_i))
                store_rows(hi_ref, l, ch, load_rows(hi_ref, l, ch) + (p_r * b_i + p_i * b_r))
                return carry

            lax.fori_loop(0, L, fix_body, 0, unroll=min(L, 8))
        else:
            fr_ref[0, :, lanes] = e_r
            fi_ref[0, :, lanes] = e_i

    if chain:
        fr_ref[0] = cr_ref[...]
        fi_ref[0] = ci_ref[...]

    ys = []
    for s in range(S5_SETS):
        h_r = jnp.concatenate([hr_ref[s * bps + q] for q in range(bps)], axis=1)
        h_i = jnp.concatenate([hi_ref[s * bps + q] for q in range(bps)], axis=1)
        ys.append(_dot(h_r, cre_ref[s]) - _dot(h_i, cim_ref[s]))
    y = jnp.concatenate(ys, axis=1) + d_ref[...] * u_ref[...]
    z = _gelu_tanh(y)
    gate = _dot(z, wglu_ref[...])
    z = z * (1.0 / (1.0 + jnp.exp(-gate)))
    o_ref[...] = x_ref[...] + g_ref[0] * _dot(z, wout_ref[...])


def s5_discretize(lam_re, lam_im, b_re, b_im, c_re, c_im, log_dt, wdtype):
    dt = jnp.exp(log_dt)[:, None]
    mag = jnp.exp(lam_re * dt)
    ar, ai = mag * jnp.cos(lam_im * dt), mag * jnp.sin(lam_im * dt)
    den = lam_re * lam_re + lam_im * lam_im
    fr = ((ar - 1.0) * lam_re + ai * lam_im) / den
    fi = (ai * lam_re - (ar - 1.0) * lam_im) / den
    bbar_re = fr[..., None] * b_re - fi[..., None] * b_im
    bbar_im = fr[..., None] * b_im + fi[..., None] * b_re
    gps = S5_GROUPS // S5_SETS
    eye = jnp.eye(gps, dtype=F32)

    def in_blocks(bbar):
        bb = bbar.reshape(S5_SETS, gps, S5_STATE, S5_GROUP)
        return jnp.einsum('ab,sapc->sacbp', eye, bb).reshape(S5_SETS, LANE, S5_SET_LANES)

    def out_blocks(c):
        cc = c.reshape(S5_SETS, gps, S5_GROUP, S5_STATE)
        return jnp.einsum('ab,sacp->sapbc', eye, cc).reshape(S5_SETS, S5_SET_LANES, LANE)

    bb = jnp.concatenate([in_blocks(bbar_re), in_blocks(bbar_im)], axis=-1).astype(wdtype)
    return (ar.reshape(1, S5_LANES), ai.reshape(1, S5_LANES), bb,
            out_blocks(c_re).astype(wdtype), out_blocks(c_im).astype(wdtype))


def s5_mixer(u, x, g, h0r, h0i, disc, d_skip, w_glu, w_out, *, n_streams, n_steps, chain, chunk_lanes,
             rows_per_batch):
    m, d = u.shape
    tm = n_streams * n_steps
    ar, ai, bb, cre, cim = disc
    nb = h0r.shape[0]
    tiles_per_batch = m // tm // nb
    s0 = h0r.shape[1]
    const2 = lambda b, i: (0, 0)
    const3 = lambda b, i: (0, 0, 0)
    row = lambda b, i: (b * tiles_per_batch + i, 0)
    kern = functools.partial(_s5_kernel, n_streams=n_streams, n_steps=n_steps, chain=chain,
                             chunk_lanes=chunk_lanes)
    return pl.pallas_call(
        kern,
        grid=(nb, tiles_per_batch),
        in_specs=[pl.BlockSpec((tm, d), row),
                  pl.BlockSpec((tm, d), row),
                  pl.BlockSpec((1, 1, d), lambda b, i: (b, 0, 0)) if g.shape[1] == 1 else
                  pl.BlockSpec((1, tm, d), lambda b, i: (0, b * tiles_per_batch + i, 0)),
                  pl.BlockSpec((1, s0, S5_LANES), lambda b, i: (b, 0, 0)),
                  pl.BlockSpec((1, s0, S5_LANES), lambda b, i: (b, 0, 0)),
                  pl.BlockSpec((1, S5_LANES), const2),
                  pl.BlockSpec((1, S5_LANES), const2),
                  pl.BlockSpec(bb.shape, const3),
                  pl.BlockSpec(cre.shape, const3),
                  pl.BlockSpec(cim.shape, const3),
                  pl.BlockSpec((1, d), const2),
                  pl.BlockSpec((d, d), const2),
                  pl.BlockSpec((d, d), const2)],
        out_specs=[pl.BlockSpec((tm, d), row),
                   pl.BlockSpec((1, s0, S5_LANES), lambda b, i: (b, 0, 0)),
                   pl.BlockSpec((1, s0, S5_LANES), lambda b, i: (b, 0, 0))],
        out_shape=[jax.ShapeDtypeStruct((m, d), F32),
                   jax.ShapeDtypeStruct(h0r.shape, F32),
                   jax.ShapeDtypeStruct(h0r.shape, F32)],
        scratch_shapes=[pltpu.VMEM((S5_LANES // LANE, tm, LANE), F32),
                        pltpu.VMEM((S5_LANES // LANE, tm, LANE), F32),
                        pltpu.VMEM((1, S5_LANES), F32), pltpu.VMEM((1, S5_LANES), F32),
                        pltpu.VMEM((n_steps, S5_LANES), F32), pltpu.VMEM((n_steps, S5_LANES), F32)],
        compiler_params=_cparams("arbitrary", "arbitrary"),
        name="s5_mixer",
    )(u, x, g, h0r, h0i, ar, ai, bb, cre, cim, d_skip.reshape(1, d), w_glu, w_out)


def _pad_cols(w, n):
    return jnp.pad(w, ((0, 0), (0, n - w.shape[1])))


def kernel(x_prompt, x_sample, cache_nsa_kv, cache_nsa_win, state_gdn_s, state_gdn_conv, state_s5_re, state_s5_im, page_table, c_prompt, c_sample, ada_w, ada_b, norm_mix, norm_ffn, norm_final, nsa_w_in, nsa_cmp_pe, nsa_cmp_w1, nsa_cmp_w2, nsa_w_out, gdn_w_in, gdn_conv_w, gdn_a_log, gdn_dt_bias, gdn_norm, gdn_w_out, s5_w_in, s5_lambda_re, s5_lambda_im, s5_b_re, s5_b_im, s5_c_re, s5_c_im, s5_d, s5_log_dt, s5_w_glu, s5_w_out, ffn_w_gu, ffn_w_down, moe_router, moe_w_gu, moe_w_down):
    Bp, Tp, d = x_prompt.shape
    Bs, Ts, _ = x_sample.shape
    Mp, Ms = Bp * Tp, Bs * Ts
    xp = x_prompt.reshape(Mp, d)
    xs = x_sample.reshape(Ms, d)

    c_all = jnp.concatenate([c_prompt, c_sample], axis=0)
    r_pad = _round_up(c_all.shape[0], 8)
    mods = adaln_all(jnp.pad(c_all, ((0, r_pad - c_all.shape[0]), (0, 0))), ada_w, ada_b)

    def mods_of(i):
        parts = jnp.split(mods[i], 6, axis=-1)
        mp = [p[:Bp].reshape(Bp, 1, d) for p in parts]
        ms = [jnp.repeat(p[Bp:Bp + Bs], Ts, axis=0).reshape(1, Ms, d) for p in parts]
        return mp, ms

    P = dict(rows_per_batch=Tp)
    S = dict(rows_per_batch=Ts)

    nsa_kv_p, nsa_kv_s, nsa_win_p, nsa_win_s = [], [], [], []
    gdn_s_p, gdn_s_s, gdn_conv_p, gdn_conv_s = [], [], [], []
    s5_re_p, s5_re_s, s5_im_p, s5_im_s = [], [], [], []

    HI = "highest"
    for i in range(DEPTH):
        (sh1_p, sc1_p, g1_p, sh2_p, sc2_p, g2_p), (sh1_s, sc1_s, g1_s, sh2_s, sc2_s, g2_s) = mods_of(i)
        j = i // N_MIXERS
        if i % N_MIXERS == 0:
            n_in = _round_up(nsa_w_in.shape[2], LANE)
            w_in = _pad_cols(nsa_w_in[j], n_in)
            w_out = nsa_w_out[j]
            proj_p = ln_matmul(xp, norm_mix[i], sc1_p, sh1_p, w_in.astype(BF16), tm=512, tn=n_in, **P)
            proj_s = ln_matmul(xs, norm_mix[i], sc1_s, sh1_s, w_in, tm=Ms, tn=n_in // 3, **S)
            kv6 = proj_p[:, NSA_Q_DIM:NSA_Q_DIM + NSA_KV_DIM].reshape(Bp, Tp, 6, NSA_KV_HEADS, HEAD_DIM)
            cmp_w = (nsa_cmp_pe[j], nsa_cmp_w1[j], nsa_cmp_w2[j])
            o_p = nsa_prompt_attention(proj_p, Bp, Tp, *cmp_w)
            kv_p, win_p = kv6[:, :, :4], kv6[:, Tp - min(WINDOW, Tp):, 4:]
            o_s, win_s = nsa_sample_attention(proj_s, Bs, Ts, cache_nsa_kv[j], cache_nsa_win[j], page_table, *cmp_w)
            kv_s = proj_s[:, NSA_Q_DIM:NSA_Q_DIM + NSA_KV_DIM].reshape(Bs, Ts, 6, NSA_KV_HEADS, HEAD_DIM)[:, :, :4]
            nsa_kv_p.append(kv_p); nsa_kv_s.append(kv_s)
            nsa_win_p.append(win_p); nsa_win_s.append(win_s)
        elif i % N_MIXERS == 1:
            n_in = _round_up(gdn_w_in.shape[2], LANE)
            w_in = _pad_cols(gdn_w_in[j], n_in)
            w_out = gdn_w_out[j]
            proj_p = ln_matmul(xp, norm_mix[i], sc1_p, sh1_p, w_in.astype(BF16), tm=256, tn=n_in, **P)
            proj_s = ln_matmul(xs, norm_mix[i], sc1_s, sh1_s, w_in, tm=Ms, tn=n_in // 3, **S)
            buf0 = jnp.zeros((Bp, CONV_WIDTH - 1, 3 * GDN_DIM), F32)
            s00 = jnp.zeros((Bp, GDN_HEADS, GDN_DK, GDN_DV), F32)
            o_p, cv_p, st_p = gdn_core(proj_p.reshape(Bp, Tp, n_in), Bp, Tp, buf0, s00, gdn_conv_w[j],
                                       gdn_a_log[j], gdn_dt_bias[j], gdn_norm[j], GDN_CHUNK)
            with jax.default_matmul_precision(HI):
                o_s, cv_s, st_s = gdn_core(proj_s.reshape(Bs, Ts, n_in), Bs, Ts, state_gdn_conv[j],
                                           state_gdn_s[j], gdn_conv_w[j], gdn_a_log[j], gdn_dt_bias[j],
                                           gdn_norm[j], Ts)
            gdn_s_p.append(st_p); gdn_s_s.append(st_s)
            gdn_conv_p.append(cv_p); gdn_conv_s.append(cv_s)
        else:
            w_in = s5_w_in[j]
            w_out = s5_w_out[j]
            u_p = ln_matmul(xp, norm_mix[i], sc1_p, sh1_p, w_in.astype(BF16), tm=512, tn=d, **P)
            u_s = ln_matmul(xs, norm_mix[i], sc1_s, sh1_s, w_in, tm=Ms, tn=d, **S)
            s5p = (s5_lambda_re[j], s5_lambda_im[j], s5_b_re[j], s5_b_im[j], s5_c_re[j], s5_c_im[j], s5_log_dt[j])
            h00 = jnp.zeros((Bp, 1, S5_LANES), F32)
            xp, re_p, im_p = s5_mixer(u_p, xp, g1_p, h00, h00, s5_discretize(*s5p, BF16), s5_d[j],
                                      s5_w_glu[j].astype(BF16), w_out.astype(BF16),
                                      n_streams=8, n_steps=32, chain=True, chunk_lanes=512, **P)
            xs, re_s, im_s = s5_mixer(u_s, xs, g1_s, state_s5_re[j].reshape(1, Bs, S5_LANES),
                                      state_s5_im[j].reshape(1, Bs, S5_LANES), s5_discretize(*s5p, F32),
                                      s5_d[j], s5_w_glu[j], w_out,
                                      n_streams=Bs, n_steps=Ts, chain=False, chunk_lanes=LANE, **S)
            st_shape = (-1, S5_GROUPS, S5_STATE)
            s5_re_p.append(re_p.reshape(st_shape)); s5_re_s.append(re_s.reshape(st_shape))
            s5_im_p.append(im_p.reshape(st_shape)); s5_im_s.append(im_s.reshape(st_shape))
        if i % N_MIXERS != 2:
            xp = matmul_residual(o_p.reshape(Mp, -1), w_out.astype(BF16), xp, g1_p, tm=512, **P)
            xs = matmul_residual(o_s.reshape(Ms, -1), w_out, xs, g1_s, tm=Ms, **S)

        f = i // 2
        if i % 2 == 0:
            w_gu, w_down = ffn_w_gu[f], ffn_w_down[f]
            xp = dense_ffn(xp, norm_ffn[i], sc2_p, sh2_p, g2_p, w_gu.astype(BF16), w_down.astype(BF16),
                           tm=1024, tf=256, **P)
            xs = dense_ffn(xs, norm_ffn[i], sc2_s, sh2_s, g2_s, w_gu, w_down, tm=Ms, tf=256, **S)
        else:
            w_gu, w_down = moe_w_gu[f], moe_w_down[f]
            w_gu_b, w_down_b = w_gu.astype(BF16), w_down.astype(BF16)
            xp = moe_layer(xp, norm_ffn[i], sc2_p, sh2_p, g2_p, moe_router[f], w_gu_b, w_down_b,
                           tm_ln=512, tm=512, tf=512, **P)
            last = i == DEPTH - 1
            xs = moe_layer(xs, norm_ffn[i], sc2_s, sh2_s, g2_s, moe_router[f],
                           w_gu_b if last else w_gu, w_down_b if last else w_down,
                           tm_ln=Ms, tm=128, tf=512, **S)

    def final_norm(x):
        return x * lax.rsqrt(jnp.mean(x * x, axis=-1, keepdims=True) + RMS_EPS) * norm_final

    y_prompt = final_norm(xp).reshape(Bp, Tp, d)
    y_sample = final_norm(xs).reshape(Bs, Ts, d)
    return (y_prompt, y_sample, jnp.stack(nsa_kv_p), jnp.stack(nsa_kv_s), jnp.stack(nsa_win_p),
            jnp.stack(nsa_win_s), jnp.stack(gdn_s_p), jnp.stack(gdn_s_s), jnp.stack(gdn_conv_p),
            jnp.stack(gdn_conv_s), jnp.stack(s5_re_p), jnp.stack(s5_re_s), jnp.stack(s5_im_p),
            jnp.stack(s5_im_s))
```

```python
import functools
import math

import jax
import jax.numpy as jnp
from jax import lax
from jax.experimental import pallas as pl
from jax.experimental.pallas import tpu as pltpu

F32 = jnp.float32
BF16 = jnp.bfloat16

D_MODEL = 1024
DEPTH = 4
PAGE_SIZE = 128
N_MIXERS = 3

NSA_HEADS = 16
HEAD_DIM = D_MODEL // NSA_HEADS
NSA_KV_HEADS = 2
NSA_HPG = NSA_HEADS // NSA_KV_HEADS
CMP_BLOCK = 32
CMP_STRIDE = 16
SEL_BLOCK = 64
SEL_TOPK = 16
WINDOW = 512
NSA_Q_BLOCK = 128
NSA_Q_DIM = NSA_HEADS * HEAD_DIM
NSA_KV_DIM = 6 * NSA_KV_HEADS * HEAD_DIM

GDN_HEADS = 8
GDN_DK = 128
GDN_DV = 128
GDN_DIM = GDN_HEADS * GDN_DK
CONV_WIDTH = 4
GDN_CHUNK = 64

S5_GROUP = 16
S5_GROUPS = D_MODEL // S5_GROUP
S5_STATE = 64

D_FF = 2816
N_EXPERTS = 8
TOP_K = 2
D_FF_EXPERT = 3584

RMS_EPS = 1e-6
NEG_INF = -1e30

LANE = 128
VMEM_LIMIT_BYTES = 56 * 1024 * 1024


def _cparams(*sem):
    return pltpu.CompilerParams(dimension_semantics=sem, vmem_limit_bytes=VMEM_LIMIT_BYTES)


def _round_up(n, m):
    return -(-n // m) * m


def _norm_mod(x, nw, sc, sh):
    y = x * lax.rsqrt(jnp.mean(x * x, axis=-1, keepdims=True) + RMS_EPS)
    return (y * nw) * (1.0 + sc) + sh


def _mod_spec(mod, tm, rows_per_batch):
    if mod.shape[1] == 1:
        return pl.BlockSpec((1, 1, mod.shape[2]), lambda i, *_: (i * tm // rows_per_batch, 0, 0))
    return pl.BlockSpec((1, tm, mod.shape[2]), lambda i, *_: (0, i, 0))


def _adaln_kernel(c_ref, w_ref, b_ref, o_ref):
    c = c_ref[...]
    s = c / (1.0 + jnp.exp(-c))
    o_ref[0] = jnp.dot(s, w_ref[0], preferred_element_type=F32, precision=lax.Precision.HIGHEST) + b_ref[0]


def adaln_all(c, ada_w, ada_b):
    r, d = c.shape
    n = ada_w.shape[2]
    tn = 1536
    return pl.pallas_call(
        _adaln_kernel,
        grid=(DEPTH, n // tn),
        in_specs=[pl.BlockSpec((r, d), lambda l, j: (0, 0)),
                  pl.BlockSpec((1, d, tn), lambda l, j: (l, 0, j)),
                  pl.BlockSpec((1, 1, tn), lambda l, j: (l, 0, j))],
        out_specs=pl.BlockSpec((1, r, tn), lambda l, j: (l, 0, j)),
        out_shape=jax.ShapeDtypeStruct((DEPTH, r, n), F32),
        compiler_params=_cparams("arbitrary", "arbitrary"),
        name="adaln",
    )(c, ada_w, ada_b.reshape(DEPTH, 1, n))


def _dot(a, b):
    prec = lax.Precision.HIGHEST if b.dtype == F32 else None
    return jnp.dot(a.astype(b.dtype), b, preferred_element_type=F32, precision=prec)


def _ln_mm_kernel(x_ref, nw_ref, sc_ref, sh_ref, w_ref, o_ref, h_ref):
    @pl.when(pl.program_id(1) == 0)
    def _():
        h_ref[...] = _norm_mod(x_ref[...], nw_ref[...], sc_ref[0], sh_ref[0]).astype(h_ref.dtype)

    o_ref[...] = _dot(h_ref[...], w_ref[...])


def ln_matmul(x, nw, sc, sh, w, *, tm, tn, rows_per_batch):
    m, d = x.shape
    n = w.shape[1]
    return pl.pallas_call(
        _ln_mm_kernel,
        grid=(m // tm, n // tn),
        in_specs=[pl.BlockSpec((tm, d), lambda i, j: (i, 0)),
                  pl.BlockSpec((1, d), lambda i, j: (0, 0)),
                  _mod_spec(sc, tm, rows_per_batch),
                  _mod_spec(sh, tm, rows_per_batch),
                  pl.BlockSpec((d, tn), lambda i, j: (0, j))],
        out_specs=pl.BlockSpec((tm, tn), lambda i, j: (i, j)),
        out_shape=jax.ShapeDtypeStruct((m, n), F32),
        scratch_shapes=[pltpu.VMEM((tm, d), w.dtype)],
        compiler_params=_cparams("arbitrary", "arbitrary"),
        name="ln_matmul",
    )(x, nw.reshape(1, d), sc, sh, w)


def _mm_res_kernel(a_ref, w_ref, x_ref, g_ref, o_ref):
    o_ref[...] = x_ref[...] + g_ref[0] * _dot(a_ref[...], w_ref[...])


def matmul_residual(a, w, x, g, *, tm, rows_per_batch):
    m, k = a.shape
    n = w.shape[1]
    return pl.pallas_call(
        _mm_res_kernel,
        grid=(m // tm,),
        in_specs=[pl.BlockSpec((tm, k), lambda i: (i, 0)),
                  pl.BlockSpec((k, n), lambda i: (0, 0)),
                  pl.BlockSpec((tm, n), lambda i: (i, 0)),
                  _mod_spec(g, tm, rows_per_batch)],
        out_specs=pl.BlockSpec((tm, n), lambda i: (i, 0)),
        out_shape=jax.ShapeDtypeStruct((m, n), F32),
        compiler_params=_cparams("arbitrary"),
        name="matmul_residual",
    )(a, w, x, g)


def _silu(a):
    return a / (1.0 + jnp.exp(-a))


def _ffn_kernel(x_ref, nw_ref, sc_ref, sh_ref, g_ref, wa_ref, wb_ref, wd_ref, o_ref, h_ref, acc_ref):
    j = pl.program_id(1)

    @pl.when(j == 0)
    def _():
        h_ref[...] = _norm_mod(x_ref[...], nw_ref[...], sc_ref[0], sh_ref[0]).astype(h_ref.dtype)
        acc_ref[...] = jnp.zeros_like(acc_ref)

    h = h_ref[...]
    a = _dot(h, wa_ref[...])
    b = _dot(h, wb_ref[...])
    acc_ref[...] += _dot(_silu(a) * b, wd_ref[...])

    @pl.when(j == pl.num_programs(1) - 1)
    def _():
        o_ref[...] = x_ref[...] + g_ref[0] * acc_ref[...]


def dense_ffn(x, nw, sc, sh, g, w_gu, w_down, *, tm, tf, rows_per_batch):
    m, d = x.shape
    ff = w_down.shape[0]
    nj = ff // tf
    return pl.pallas_call(
        _ffn_kernel,
        grid=(m // tm, nj),
        in_specs=[pl.BlockSpec((tm, d), lambda i, j: (i, 0)),
                  pl.BlockSpec((1, d), lambda i, j: (0, 0)),
                  _mod_spec(sc, tm, rows_per_batch),
                  _mod_spec(sh, tm, rows_per_batch),
                  _mod_spec(g, tm, rows_per_batch),
                  pl.BlockSpec((d, tf), lambda i, j: (0, j)),
                  pl.BlockSpec((d, tf), lambda i, j: (0, nj + j)),
                  pl.BlockSpec((tf, d), lambda i, j: (j, 0))],
        out_specs=pl.BlockSpec((tm, d), lambda i, j: (i, 0)),
        out_shape=jax.ShapeDtypeStruct((m, d), F32),
        scratch_shapes=[pltpu.VMEM((tm, d), w_gu.dtype), pltpu.VMEM((tm, d), F32)],
        compiler_params=_cparams("arbitrary", "arbitrary"),
        name="dense_ffn",
    )(x, nw.reshape(1, d), sc, sh, g, w_gu, w_gu, w_down)


def _ln_router_kernel(x_ref, nw_ref, sc_ref, sh_ref, wr_ref, h_ref, lg_ref):
    h = _norm_mod(x_ref[...], nw_ref[...], sc_ref[0], sh_ref[0])
    h_ref[...] = h.astype(h_ref.dtype)
    lg_ref[...] = jnp.dot(h, wr_ref[...], preferred_element_type=F32, precision=lax.Precision.HIGHEST)


def ln_router(x, nw, sc, sh, w_router, *, tm, rows_per_batch, h_dtype):
    m, d = x.shape
    wr = jnp.pad(w_router, ((0, 0), (0, LANE - N_EXPERTS)))
    h, lg = pl.pallas_call(
        _ln_router_kernel,
        grid=(m // tm,),
        in_specs=[pl.BlockSpec((tm, d), lambda i: (i, 0)),
                  pl.BlockSpec((1, d), lambda i: (0, 0)),
                  _mod_spec(sc, tm, rows_per_batch),
                  _mod_spec(sh, tm, rows_per_batch),
                  pl.BlockSpec((d, LANE), lambda i: (0, 0))],
        out_specs=[pl.BlockSpec((tm, d), lambda i: (i, 0)),
                   pl.BlockSpec((tm, LANE), lambda i: (i, 0))],
        out_shape=[jax.ShapeDtypeStruct((m, d), h_dtype), jax.ShapeDtypeStruct((m, LANE), F32)],
        compiler_params=_cparams("arbitrary"),
        name="ln_router",
    )(x, nw.reshape(1, d), sc, sh, wr)
    return h, lg[:, :N_EXPERTS]


def _moe_kernel(be_ref, nu_ref, r_ref, wa_ref, wb_ref, wd_ref, o_ref, acc_ref):
    i = pl.program_id(0)
    j = pl.program_id(1)

    @pl.when(i < nu_ref[0])
    def _():
        @pl.when(j == 0)
        def _():
            acc_ref[...] = jnp.zeros_like(acc_ref)

        r = r_ref[...]
        a = _dot(r, wa_ref[0])
        b = _dot(r, wb_ref[0])
        acc_ref[...] += _dot(_silu(a) * b, wd_ref[0])

        @pl.when(j == pl.num_programs(1) - 1)
        def _():
            o_ref[...] = acc_ref[...]


def moe_experts(rows, block_e, n_used, w_gu, w_down, *, tm, tf):
    m, d = rows.shape
    ff = w_down.shape[1]
    nj = ff // tf
    n_blocks = m // tm

    def blk(i, nu):
        return jnp.minimum(i, nu[0] - 1)

    def jj(i, j, nu):
        return jnp.where(i < nu[0], j, nj - 1)

    grid_spec = pltpu.PrefetchScalarGridSpec(
        num_scalar_prefetch=2,
        grid=(n_blocks, nj),
        in_specs=[pl.BlockSpec((tm, d), lambda i, j, be, nu: (blk(i, nu), 0)),
                  pl.BlockSpec((1, d, tf), lambda i, j, be, nu: (be[blk(i, nu)], 0, jj(i, j, nu))),
                  pl.BlockSpec((1, d, tf), lambda i, j, be, nu: (be[blk(i, nu)], 0, nj + jj(i, j, nu))),
                  pl.BlockSpec((1, tf, d), lambda i, j, be, nu: (be[blk(i, nu)], jj(i, j, nu), 0))],
        out_specs=pl.BlockSpec((tm, d), lambda i, j, be, nu: (blk(i, nu), 0)),
        scratch_shapes=[pltpu.VMEM((tm, d), F32)],
    )
    return pl.pallas_call(
        _moe_kernel,
        grid_spec=grid_spec,
        out_shape=jax.ShapeDtypeStruct((m, d), F32),
        compiler_params=_cparams("arbitrary", "arbitrary"),
        name="moe_experts",
    )(block_e, n_used, rows, w_gu, w_gu, w_down)


def moe_layer(x, nw, sc, sh, g, w_router, w_gu, w_down, *, tm_ln, tm, tf, rows_per_batch):
    n_tok, d = x.shape
    n_rows = n_tok * TOP_K
    h, logits = ln_router(x, nw, sc, sh, w_router, tm=tm_ln, rows_per_batch=rows_per_batch, h_dtype=w_gu.dtype)
    top_logit, top_e = lax.top_k(logits, TOP_K)
    gate = jax.nn.softmax(top_logit, axis=-1)
    flat_e = top_e.reshape(-1)
    order = jnp.argsort(flat_e)
    e_sorted = flat_e[order]
    tok_sorted = order // TOP_K
    counts = jnp.bincount(flat_e, length=N_EXPERTS)
    padded = (counts + tm - 1) // tm * tm
    pad_end = jnp.cumsum(padded)
    pad_start = pad_end - padded
    grp_start = jnp.cumsum(counts) - counts
    dest = pad_start[e_sorted] + jnp.arange(n_rows) - grp_start[e_sorted]
    n_blocks = -(-n_rows // tm) + N_EXPERTS
    src = jnp.zeros((n_blocks * tm,), jnp.int32).at[dest].set(tok_sorted.astype(jnp.int32))
    rows = h[src]
    block_e = jnp.minimum(jnp.searchsorted(pad_end, jnp.arange(n_blocks) * tm, side='right'),
                          N_EXPERTS - 1).astype(jnp.int32)
    n_used = (pad_end[-1] // tm).astype(jnp.int32).reshape(1)
    out = moe_experts(rows, block_e, n_used, w_gu, w_down, tm=tm, tf=tf)
    pos = jnp.zeros((n_rows,), jnp.int32).at[order].set(dest.astype(jnp.int32)).reshape(n_tok, TOP_K)
    f = out[pos[:, 0]] * gate[:, 0:1] + out[pos[:, 1]] * gate[:, 1:2]
    if g.shape[1] == 1:
        gg = jnp.repeat(g[:, 0], rows_per_batch, axis=0)
    else:
        gg = g[0]
    return x + gg * f


def alibi_slopes():
    return jnp.exp2(-8.0 * (jnp.arange(NSA_HEADS, dtype=F32) + 1.0) / NSA_HEADS)


def nsa_split(proj, B, T):
    q = proj[..., :NSA_Q_DIM].reshape(B, T, NSA_HEADS, HEAD_DIM)
    kv = proj[..., NSA_Q_DIM:NSA_Q_DIM + NSA_KV_DIM].reshape(B, T, 6, NSA_KV_HEADS, HEAD_DIM)
    gates = jax.nn.sigmoid(proj[..., NSA_Q_DIM + NSA_KV_DIM:NSA_Q_DIM + NSA_KV_DIM + 3 * NSA_HEADS]
                           ).reshape(B, T, NSA_HEADS, 3)
    return q, kv, gates


def nsa_compress(seq, pe, w1, w2):
    B, L, G, dh = seq.shape
    r = CMP_BLOCK // CMP_STRIDE
    n_chunk = L // CMP_STRIDE
    nc = n_chunk - r + 1
    ch = seq.reshape(B, n_chunk, CMP_STRIDE, G, dh)
    blocks = jnp.concatenate([ch[:, j:j + nc] for j in range(r)], axis=2)
    blocks = blocks + pe[None, None, :, None, :]
    flat = blocks.transpose(0, 1, 3, 2, 4).reshape(B, nc, G, CMP_BLOCK * dh)
    return jax.nn.gelu(flat @ w1) @ w2


def cmp_to_sel_map(nc, nsb):
    c_start = jnp.arange(nc) * CMP_STRIDE
    s_start = jnp.arange(nsb) * SEL_BLOCK
    hit = (c_start[:, None] < s_start[None, :] + SEL_BLOCK) & (c_start[:, None] + CMP_BLOCK > s_start[None, :])
    return hit.astype(F32)


def nsa_attend_block(q, gates, q_pos, kc, vc, kc_pos, ks_t, vs_t, kw, vw, kw_pos):
    B, Tq = q.shape[:2]
    nc = kc.shape[1]
    nsb = ks_t.shape[2]
    n_sel = min(SEL_TOPK, nsb)
    slopes = alibi_slopes().reshape(NSA_KV_HEADS, NSA_HPG)
    qg = q.reshape(B, Tq, NSA_KV_HEADS, NSA_HPG, HEAD_DIM) * (HEAD_DIM ** -0.5)

    d_c = q_pos[:, None] - kc_pos[None, :]
    ok_c = (d_c >= 0)[None, :, None, None, :]
    s_c = jnp.einsum('btghd,bngd->btghn', qg, kc).astype(F32)
    s_c = s_c - slopes[None, None, :, :, None] * jnp.abs(d_c).astype(F32)[None, :, None, None, :]
    s_c = jnp.where(ok_c, s_c, NEG_INF)
    p_c = jax.nn.softmax(s_c, axis=-1) * ok_c
    o_c = jnp.einsum('btghn,bngd->btghd', p_c.astype(vc.dtype), vc)

    imp = jnp.einsum('btghn,nj->btgj', p_c, cmp_to_sel_map(nc, nsb))
    cur = q_pos // SEL_BLOCK
    blk = jnp.arange(nsb)
    is_cur = (blk[None, :] == cur[:, None])[None, :, None, :]
    is_past = (blk[None, :] < cur[:, None])[None, :, None, :]
    imp = jnp.where(is_cur, jnp.inf, jnp.where(is_past, imp, -jnp.inf))
    _, idx = lax.top_k(imp, n_sel)
    bi = jnp.arange(B)[:, None, None, None]
    gi = jnp.arange(NSA_KV_HEADS)[None, None, :, None]
    k_sel = ks_t[bi, gi, idx]
    v_sel = vs_t[bi, gi, idx]
    pos_s = idx[..., None] * SEL_BLOCK + jnp.arange(SEL_BLOCK)
    d_s = q_pos[None, :, None, None, None] - pos_s
    ok_s = (d_s >= 0)[:, :, :, None]
    s_s = jnp.einsum('btghd,btgnkd->btghnk', qg, k_sel).astype(F32)
    s_s = s_s - slopes[None, None, :, :, None, None] * jnp.abs(d_s).astype(F32)[:, :, :, None]
    s_s = jnp.where(ok_s, s_s, NEG_INF).reshape(B, Tq, NSA_KV_HEADS, NSA_HPG, n_sel * SEL_BLOCK)
    p_s = jax.nn.softmax(s_s, axis=-1).reshape(B, Tq, NSA_KV_HEADS, NSA_HPG, n_sel, SEL_BLOCK)
    o_s = jnp.einsum('btghnk,btgnkd->btghd', p_s.astype(v_sel.dtype), v_sel)

    d_w = q_pos[:, None] - kw_pos[None, :]
    ok_w = ((d_w >= 0) & (d_w < WINDOW) & (kw_pos >= 0)[None, :])[None, :, None, None, :]
    s_w = jnp.einsum('btghd,blgd->btghl', qg, kw).astype(F32)
    s_w = s_w - slopes[None, None, :, :, None] * jnp.abs(d_w).astype(F32)[None, :, None, None, :]
    p_w = jax.nn.softmax(jnp.where(ok_w, s_w, NEG_INF), axis=-1)
    o_w = jnp.einsum('btghl,blgd->btghd', p_w.astype(vw.dtype), vw)

    g = gates.reshape(B, Tq, NSA_KV_HEADS, NSA_HPG, 3).astype(q.dtype)
    o = g[..., 0:1] * o_c + g[..., 1:2] * o_s + g[..., 2:3] * o_w
    return o.reshape(B, Tq, NSA_Q_DIM)


def nsa_prompt_core(proj, B, T, pe, w1, w2):
    q, kv, gates = nsa_split(proj, B, T)
    kc = nsa_compress(kv[:, :, 0], pe[0], w1[0], w2[0])
    vc = nsa_compress(kv[:, :, 1], pe[1], w1[1], w2[1])
    kc_pos = jnp.arange(kc.shape[1]) * CMP_STRIDE + CMP_BLOCK - 1
    nsb = T // SEL_BLOCK
    ks_t = kv[:, :, 2].reshape(B, nsb, SEL_BLOCK, NSA_KV_HEADS, HEAD_DIM).transpose(0, 3, 1, 2, 4)
    vs_t = kv[:, :, 3].reshape(B, nsb, SEL_BLOCK, NSA_KV_HEADS, HEAD_DIM).transpose(0, 3, 1, 2, 4)
    win_pad = jnp.pad(kv[:, :, 4:], ((0, 0), (WINDOW, 0), (0, 0), (0, 0), (0, 0)))

    def q_block(i):
        start = i * NSA_Q_BLOCK
        qb = lax.dynamic_slice_in_dim(q, start, NSA_Q_BLOCK, axis=1)
        gb = lax.dynamic_slice_in_dim(gates, start, NSA_Q_BLOCK, axis=1)
        wb = lax.dynamic_slice_in_dim(win_pad, start, WINDOW + NSA_Q_BLOCK, axis=1)
        q_pos = start + jnp.arange(NSA_Q_BLOCK)
        kw_pos = start - WINDOW + jnp.arange(WINDOW + NSA_Q_BLOCK)
        return nsa_attend_block(qb, gb, q_pos, kc, vc, kc_pos, ks_t, vs_t, wb[:, :, 0], wb[:, :, 1], kw_pos)

    o = lax.map(q_block, jnp.arange(T // NSA_Q_BLOCK))
    o = o.transpose(1, 0, 2, 3).reshape(B, T, NSA_Q_DIM)
    return o, kv[:, :, :4], kv[:, T - min(WINDOW, T):, 4:]


def nsa_sample_core(proj, B, T, cache_kv, cache_win, page_table, pe, w1, w2):
    past_len = page_table.shape[1] * cache_kv.shape[1]
    win_len = cache_win.shape[1]
    q, kv, gates = nsa_split(proj, B, T)
    past = cache_kv[page_table].reshape(B, past_len, 4, NSA_KV_HEADS, HEAD_DIM)
    full = jnp.concatenate([past, kv[:, :, :4]], axis=1)
    L = past_len + T
    Lp = -(-L // SEL_BLOCK) * SEL_BLOCK
    full = jnp.pad(full, ((0, 0), (0, Lp - L), (0, 0), (0, 0), (0, 0)))
    kc = nsa_compress(full[:, :, 0], pe[0], w1[0], w2[0])
    vc = nsa_compress(full[:, :, 1], pe[1], w1[1], w2[1])
    kc_pos = jnp.arange(kc.shape[1]) * CMP_STRIDE + CMP_BLOCK - 1
    nsb = Lp // SEL_BLOCK
    ks_t = full[:, :, 2].reshape(B, nsb, SEL_BLOCK, NSA_KV_HEADS, HEAD_DIM).transpose(0, 3, 1, 2, 4)
    vs_t = full[:, :, 3].reshape(B, nsb, SEL_BLOCK, NSA_KV_HEADS, HEAD_DIM).transpose(0, 3, 1, 2, 4)
    win = jnp.concatenate([cache_win, kv[:, :, 4:]], axis=1)
    kw_pos = past_len - win_len + jnp.arange(win_len + T)
    q_pos = past_len + jnp.arange(T)
    o = nsa_attend_block(q, gates, q_pos, kc, vc, kc_pos, ks_t, vs_t, win[:, :, 0], win[:, :, 1], kw_pos)
    return o, kv[:, :, :4], win[:, win.shape[1] - win_len:]


NSA_TQ = 128
NSA_TK = 512


def _split3(x):
    hi = x.astype(BF16)
    return hi, (x - hi.astype(F32)).astype(BF16)


def _dot3(a, b, dims):
    dn = (dims, ((), ()))
    if b.dtype == BF16:
        return lax.dot_general(a.astype(BF16), b, dn, preferred_element_type=F32)
    a_hi, a_lo = _split3(a.astype(F32))
    b_hi, b_lo = _split3(b)
    return (lax.dot_general(a_hi, b_hi, dn, preferred_element_type=F32)
            + (lax.dot_general(a_hi, b_lo, dn, preferred_element_type=F32)
               + lax.dot_general(a_lo, b_hi, dn, preferred_element_type=F32)))


def _qk(q, k):
    return _dot3(q, k, ((1,), (1,)))


def _pv(p, v):
    return _dot3(p, v, ((1,), (0,)))


def _nsa_attn_kernel(*refs, tq, q_start, n_past_tiles, n_cmp, nsb, win_keys, win_len, win_start, tail_start):
    if tail_start is None:
        (q_ref, kc_ref, vc_ref, ks_ref, vs_ref, kw_ref, vw_ref, gate_ref, slope_ref, map_ref, e_ref, o_ref) = refs
    else:
        (q_ref, kc_ref, vc_ref, ks_ref, vs_ref, kw_ref, vw_ref, gate_ref, slope_ref, map_ref, e_ref,
         kt_ref, vt_ref, o_ref) = refs
    g = pl.program_id(1)
    i = pl.program_id(2)
    TQ, TK, H = tq, NSA_TK, NSA_HPG
    R = H * TQ
    start = q_start + i * TQ
    q = q_ref[0, 0].reshape(R, HEAD_DIM)
    slope_col = jnp.concatenate([jnp.broadcast_to(slope_ref[0, h][:, 0:1], (TQ, 1)) for h in range(H)], axis=0)
    qpos = start + lax.broadcasted_iota(jnp.int32, (TQ, 1), 0)
    qpos_r = jnp.concatenate([qpos] * H, axis=0)

    kcpos = lax.broadcasted_iota(jnp.int32, (1, n_cmp), 1) * CMP_STRIDE + (CMP_BLOCK - 1)
    ok_c = kcpos <= qpos_r
    s = _qk(q, kc_ref[0, 0]) + slope_col * (kcpos - start).astype(F32)
    s = jnp.where(ok_c, s, NEG_INF)
    e = jnp.exp(s - jnp.max(s, axis=-1, keepdims=True))
    pn = jnp.where(ok_c, e, 0.0) * (1.0 / jnp.sum(e, axis=-1, keepdims=True))
    o_c = _pv(pn, vc_ref[0, 0])
    p_grp = pn[0:TQ]
    for h in range(1, H):
        p_grp = p_grp + pn[h * TQ:(h + 1) * TQ]
    imp = jnp.dot(p_grp, map_ref[...], preferred_element_type=F32, precision=lax.Precision.HIGHEST)

    blk = lax.broadcasted_iota(jnp.int32, (1, nsb), 1)
    blk_f = blk.astype(F32)
    cur = qpos // SEL_BLOCK
    work = jnp.where(blk < cur, imp, -1.0)
    sel = jnp.where(blk == cur, 1.0, 0.0)
    for _ in range(min(SEL_TOPK, nsb) - 1):
        mx = jnp.max(work, axis=-1, keepdims=True)
        first = jnp.min(jnp.where(work == mx, blk_f, float(nsb)), axis=-1, keepdims=True)
        pick = jnp.logical_and(blk_f == first, mx >= 0.0)
        sel = jnp.where(pick, 1.0, sel)
        work = jnp.where(pick, -1.0, work)
    sel_b = sel.astype(BF16)

    def online_step(carry, k, v, kpos, mb):
        m, l, acc = carry
        s = _qk(q, k) + slope_col * (kpos - start).astype(F32)
        s = s + jnp.concatenate([mb] * H, axis=0)
        m_new = jnp.maximum(m, jnp.max(s, axis=-1, keepdims=True))
        alpha = jnp.exp(m - m_new)
        p = jnp.exp(s - m_new)
        l = alpha * l + jnp.sum(p, axis=-1, keepdims=True)
        return m_new, l, alpha * acc + _pv(p, v)

    def sel_tile(j, carry, diagonal):
        off = pl.multiple_of(j * TK, TK)
        kpos = off + lax.broadcasted_iota(jnp.int32, (1, TK), 1)
        mb = (jnp.dot(sel_b, e_ref[j], preferred_element_type=F32) - 1.0) * (-NEG_INF)
        if diagonal:
            mb = jnp.where(kpos <= qpos, mb, NEG_INF)
        return online_step(carry, ks_ref[0, 0, pl.ds(off, TK), :], vs_ref[0, 0, pl.ds(off, TK), :], kpos, mb)

    init = (jnp.full((R, 1), NEG_INF, F32), jnp.zeros((R, 1), F32), jnp.zeros((R, HEAD_DIM), F32))
    if tail_start is None:
        j_diag = start // TK
        carry = lax.fori_loop(0, j_diag, lambda j, c: sel_tile(j, c, False), init)
        _, l_s, acc_s = sel_tile(j_diag, carry, True)
    else:
        carry = lax.fori_loop(0, n_past_tiles, lambda j, c: sel_tile(j, c, False), init)
        n_tail = kt_ref.shape[2]
        ktpos = tail_start + lax.broadcasted_iota(jnp.int32, (1, n_tail), 1)
        mb = jnp.where(ktpos <= qpos, 0.0, NEG_INF)
        _, l_s, acc_s = online_step(carry, kt_ref[0, 0], vt_ref[0, 0], ktpos, mb)
    o_s = acc_s * (1.0 / l_s)

    if tail_start is None:
        ws = pl.multiple_of(jnp.maximum(start - win_len, 0), TQ)
        kw, vw = kw_ref[0, 0, pl.ds(ws, win_keys), :], vw_ref[0, 0, pl.ds(ws, win_keys), :]
    else:
        ws = win_start
        kw, vw = kw_ref[0, 0], vw_ref[0, 0]
    kwpos = ws + lax.broadcasted_iota(jnp.int32, (1, win_keys), 1)
    d_w = qpos_r - kwpos
    ok_w = jnp.logical_and(d_w >= 0, d_w < win_len)
    s = _qk(q, kw) + slope_col * (kwpos - start).astype(F32)
    s = jnp.where(ok_w, s, NEG_INF)
    e = jnp.exp(s - jnp.max(s, axis=-1, keepdims=True))
    o_w = _pv(e, vw) * (1.0 / jnp.sum(e, axis=-1, keepdims=True))

    graw = gate_ref[0]
    gsel = jnp.where(g == 0, graw[:, 0:3 * H], graw[:, 3 * H:6 * H])
    sig = 1.0 / (1.0 + jnp.exp(-gsel))
    for h in range(H):
        rows = slice(h * TQ, (h + 1) * TQ)
        o_h = (sig[:, 3 * h:3 * h + 1] * o_c[rows] + sig[:, 3 * h + 1:3 * h + 2] * o_s[rows]
               + sig[:, 3 * h + 2:3 * h + 3] * o_w[rows])
        o_ref[0, 0, h] = o_h.astype(o_ref.dtype)


LOG2E = 1.4426950408889634


def _nsa_prompt_kernel(q_ref, kc_ref, vc_ref, ks_ref, vs_ref, kw_ref, vw_ref, gate_ref, map_ref, o_ref,
                       *, n_cmp, nsb, win_keys, win_len):
    g = pl.program_id(1)
    i = pl.program_id(2)
    TQ, TK, H, dh = NSA_TQ, NSA_TK, NSA_HPG, HEAD_DIM
    R = H * TQ
    start = i * TQ
    qa = q_ref[0, 0].reshape(R, LANE)
    qpos = start + lax.broadcasted_iota(jnp.int32, (TQ, 1), 0)
    qpos_r = jnp.concatenate([qpos] * H, axis=0)

    kcpos = lax.broadcasted_iota(jnp.int32, (1, n_cmp), 1) * CMP_STRIDE + (CMP_BLOCK - 1)
    ok_c = kcpos <= qpos_r
    s = jnp.where(ok_c, _qk(qa, kc_ref[0, 0]), NEG_INF)
    e = jnp.exp2(s - jnp.max(s, axis=-1, keepdims=True))
    pn = jnp.where(ok_c, e, 0.0) * (1.0 / jnp.sum(e, axis=-1, keepdims=True))
    o_c = _pv(pn, vc_ref[0, 0])
    p_grp = pn[0:TQ]
    for h in range(1, H):
        p_grp = p_grp + pn[h * TQ:(h + 1) * TQ]
    p1 = p_grp.astype(BF16)
    r1 = p_grp - p1.astype(F32)
    p2 = r1.astype(BF16)
    p3 = (r1 - p2.astype(F32)).astype(BF16)
    cmap = map_ref[...]
    imp = (jnp.dot(p1, cmap, preferred_element_type=F32)
           + (jnp.dot(p2, cmap, preferred_element_type=F32) + jnp.dot(p3, cmap, preferred_element_type=F32)))

    ws = pl.multiple_of(jnp.maximum(start - win_len, 0), TQ)
    kwpos = ws + lax.broadcasted_iota(jnp.int32, (1, win_keys), 1)
    d_w = qpos_r - kwpos
    ok_w = jnp.logical_and(d_w >= 0, d_w < win_len)
    s = jnp.where(ok_w, _qk(qa, kw_ref[0, 0, pl.ds(ws, win_keys), :]), NEG_INF)
    e = jnp.exp2(s - jnp.max(s, axis=-1, keepdims=True))
    acc_w = _pv(e, vw_ref[0, 0, pl.ds(ws, win_keys), :])
    o_w = acc_w[:, 0:dh] * (1.0 / acc_w[:, dh:dh + 1])

    blk = lax.broadcasted_iota(jnp.int32, (1, nsb), 1)
    blk_f = blk.astype(F32)
    cur = qpos // SEL_BLOCK
    work = jnp.where(blk < cur, imp, -1.0)
    sel = jnp.where(blk == cur, 1.0, 0.0)
    for _ in range(min(SEL_TOPK, nsb) - 1):
        mx = jnp.max(work, axis=-1, keepdims=True)
        first = jnp.min(jnp.where(work == mx, blk_f, float(nsb)), axis=-1, keepdims=True)
        pick = jnp.logical_and(blk_f == first, mx >= 0.0)
        sel = jnp.where(pick, 1.0, sel)
        work = jnp.where(pick, -1.0, work)
    neg_sel = ((sel - 1.0) * (-NEG_INF)).astype(qa.dtype)
    q_full = jnp.concatenate([qa, jnp.concatenate([neg_sel] * H, axis=0)], axis=1)

    def sel_tile(j, carry, diagonal):
        m, acc = carry
        off = pl.multiple_of(j * TK, TK)
        s = _qk(q_full, ks_ref[0, 0, pl.ds(off, TK), :])
        if diagonal:
            kpos = off + lax.broadcasted_iota(jnp.int32, (1, TK), 1)
            s = jnp.where(kpos <= qpos_r, s, NEG_INF)
        m_new = jnp.maximum(m, jnp.max(s, axis=-1, keepdims=True))
        p = jnp.exp2(s - m_new)
        return m_new, jnp.exp2(m - m_new) * acc + _pv(p, vs_ref[0, 0, pl.ds(off, TK), :])

    init = (jnp.full((R, 1), NEG_INF, F32), jnp.zeros((R, LANE), F32))
    j_diag = start // TK
    carry = lax.fori_loop(0, j_diag, lambda j, c: sel_tile(j, c, False), init)
    _, acc_s = sel_tile(j_diag, carry, True)
    o_s = acc_s[:, 0:dh] * (1.0 / acc_s[:, dh:dh + 1])

    graw = gate_ref[0]
    gsel = jnp.where(g == 0, graw[:, 0:3 * H], graw[:, 3 * H:6 * H])
    sig = 1.0 / (1.0 + jnp.exp(-gsel))
    for h in range(H):
        rows = slice(h * TQ, (h + 1) * TQ)
        o_h = (sig[:, 3 * h:3 * h + 1] * o_c[rows] + sig[:, 3 * h + 1:3 * h + 2] * o_s[rows]
               + sig[:, 3 * h + 2:3 * h + 3] * o_w[rows])
        o_ref[0, 0, h] = o_h.astype(o_ref.dtype)


def _nsa_cmp_kernel(x_ref, pe_ref, w1_ref, w2_ref, o_ref):
    x = x_ref[0, 0]
    n, half = x.shape
    a = _dot3(x + pe_ref[0, 0:1, :], w1_ref[0, 0:half, :], ((1,), (0,)))
    b = _dot3(x + pe_ref[0, 1:2, :], w1_ref[0, half:, :], ((1,), (0,)))
    pre = a + pltpu.roll(b, n - 1, 0)
    o_ref[0, 0] = _dot3(_gelu_tanh(pre), w2_ref[0], ((1,), (0,))).astype(o_ref.dtype)


def nsa_compress_blocks(x, pe, w1, w2, out_dtype):
    kv, n, n_chunk, flat = x.shape
    hid = w1.shape[2]
    dh = w2.shape[2]
    return pl.pallas_call(
        _nsa_cmp_kernel,
        grid=(kv, n),
        in_specs=[pl.BlockSpec((1, 1, n_chunk, flat), lambda c, i: (c, i, 0, 0)),
                  pl.BlockSpec((1, 2, flat), lambda c, i: (c, 0, 0)),
                  pl.BlockSpec((1, 2 * flat, hid), lambda c, i: (c, 0, 0)),
                  pl.BlockSpec((1, hid, dh), lambda c, i: (c, 0, 0))],
        out_specs=pl.BlockSpec((1, 1, n_chunk, dh), lambda c, i: (c, i, 0, 0)),
        out_shape=jax.ShapeDtypeStruct((kv, n, n_chunk, dh), out_dtype),
        compiler_params=_cparams("arbitrary", "arbitrary"),
        name="nsa_compress",
    )(x, pe.reshape(kv, 2, flat).astype(F32), w1.astype(x.dtype), w2.astype(x.dtype))


def _nsa_consts(n_cmp, nsb, n_tiles):
    nsb_pad = _round_up(nsb, LANE)
    slopes = jnp.broadcast_to(alibi_slopes().reshape(NSA_KV_HEADS, NSA_HPG, 1, 1), (NSA_KV_HEADS, NSA_HPG, 1, LANE))
    cmap = jnp.pad(cmp_to_sel_map(n_cmp, nsb), ((0, 0), (0, nsb_pad - nsb)))
    key_blk = jnp.arange(n_tiles * NSA_TK, dtype=jnp.int32) // SEL_BLOCK
    expand = (jnp.arange(nsb_pad, dtype=jnp.int32)[:, None] == key_blk[None, :]).astype(BF16)
    expand = expand.reshape(nsb_pad, n_tiles, NSA_TK).transpose(1, 0, 2)
    return slopes, cmap, expand, nsb_pad


def nsa_prompt_attention(proj, B, T, pe, w1, w2, cdt=BF16):
    assert NSA_KV_HEADS == 2 and T % NSA_TK == 0 and NSA_TK % NSA_TQ == 0 and NSA_TQ % SEL_BLOCK == 0
    G, H, dh = NSA_KV_HEADS, NSA_HPG, HEAD_DIM
    nq = T // NSA_TQ
    n_cmp = T // CMP_STRIDE
    nsb = T // SEL_BLOCK
    n_in = proj.shape[1]
    q5 = (proj[:, :NSA_Q_DIM] * (dh ** -0.5 * LOG2E)).astype(cdt).reshape(B, T, G, H, dh).transpose(0, 2, 3, 1, 4)
    kv = proj[:, NSA_Q_DIM:NSA_Q_DIM + NSA_KV_DIM].astype(cdt).reshape(B, T, 6, G, dh)
    xcmp = kv[:, :, 0:2].reshape(B, n_cmp, CMP_STRIDE, 2, G, dh).transpose(3, 0, 4, 1, 2, 5)
    kvc = nsa_compress_blocks(xcmp.reshape(2, B * G, n_cmp, CMP_STRIDE * dh), pe, w1, w2, cdt)
    kvc = kvc.reshape(2, B, G, n_cmp, dh)
    kvt = kv.transpose(2, 0, 3, 1, 4)
    _, cmap, _, nsb_pad = _nsa_consts(n_cmp, nsb, T // NSA_TK)
    assert nsb_pad == LANE

    sl = alibi_slopes() * LOG2E
    s1 = sl.astype(BF16)
    s2 = (sl - s1.astype(F32)).astype(BF16)
    s3 = (sl - s1.astype(F32) - s2.astype(F32)).astype(BF16)
    q_cols = jnp.stack([s1, s2, s3, s1, s2, s3], axis=-1).astype(cdt).reshape(1, G, H, 1, 6)

    def pos_cols(pos):
        hi = (pos // SEL_BLOCK * SEL_BLOCK).astype(cdt)
        lo = (pos % SEL_BLOCK).astype(cdt)
        return jnp.stack([hi, hi, hi, lo, lo, lo], axis=-1)

    def with_cols(x, cols, width):
        n = x.shape[2]
        parts = [x, jnp.broadcast_to(cols, (B, G, n, cols.shape[-1]))]
        parts.append(jnp.zeros((B, G, n, width - dh - cols.shape[-1]), cdt))
        return jnp.concatenate(parts, axis=-1)

    pos = jnp.arange(T, dtype=jnp.int32)
    kcols = pos_cols(pos)
    onehot = (pos[:, None] // SEL_BLOCK == jnp.arange(nsb_pad, dtype=jnp.int32)[None, :]).astype(cdt)
    q_aug = jnp.concatenate([q5, jnp.broadcast_to(q_cols, (B, G, H, T, 6)),
                             jnp.zeros((B, G, H, T, LANE - dh - 6), cdt)], axis=-1)
    ks_aug = jnp.concatenate([with_cols(kvt[2], kcols, LANE),
                              jnp.broadcast_to(onehot, (B, G, T, nsb_pad))], axis=-1)
    kw_aug = with_cols(kvt[4], kcols, LANE)
    kc_aug = with_cols(kvc[0], pos_cols(jnp.arange(n_cmp, dtype=jnp.int32) * CMP_STRIDE + (CMP_BLOCK - 1)), LANE)
    ones_col = jnp.ones((1, 1, 1, 1), cdt)
    vs_aug = with_cols(kvt[3], ones_col, LANE)
    vw_aug = with_cols(kvt[5], ones_col, LANE)

    gate_col = (NSA_Q_DIM + NSA_KV_DIM) // LANE
    seq = lambda w: pl.BlockSpec((1, 1, T, w), lambda b, g, i: (b, g, 0, 0))
    kern = functools.partial(_nsa_prompt_kernel, n_cmp=n_cmp, nsb=nsb_pad, win_keys=WINDOW + NSA_TQ, win_len=WINDOW)
    o5 = pl.pallas_call(
        kern,
        grid=(B, G, nq),
        in_specs=[pl.BlockSpec((1, 1, H, NSA_TQ, LANE), lambda b, g, i: (b, g, 0, i, 0)),
                  pl.BlockSpec((1, 1, n_cmp, LANE), lambda b, g, i: (b, g, 0, 0)),
                  pl.BlockSpec((1, 1, n_cmp, dh), lambda b, g, i: (b, g, 0, 0)),
                  seq(2 * LANE), seq(LANE), seq(LANE), seq(LANE),
                  pl.BlockSpec((1, NSA_TQ, LANE), lambda b, g, i: (b * nq + i, 0, gate_col)),
                  pl.BlockSpec(cmap.shape, lambda b, g, i: (0, 0))],
        out_specs=pl.BlockSpec((1, 1, H, NSA_TQ, dh), lambda b, g, i: (b, g, 0, i, 0)),
        out_shape=jax.ShapeDtypeStruct((B, G, H, T, dh), cdt),
        compiler_params=_cparams("arbitrary", "arbitrary", "arbitrary"),
        name="nsa_attention",
    )(q_aug, kc_aug, kvc[1], ks_aug, vs_aug, kw_aug, vw_aug, proj.reshape(B * nq, NSA_TQ, n_in),
      cmap.astype(BF16))
    return o5.transpose(0, 3, 1, 2, 4).reshape(B * T, G * H * dh)


NSA_TQ_DECODE = 8


def nsa_sample_attention(proj, B, T, cache_kv, cache_win, page_table, pe, w1, w2):
    G, H, dh, TQ = NSA_KV_HEADS, NSA_HPG, HEAD_DIM, NSA_TQ_DECODE
    n_pages, page = page_table.shape[1], cache_kv.shape[1]
    past = n_pages * page
    win_len = cache_win.shape[1]
    last_q = past + T - 1
    n_cmp = past // CMP_STRIDE
    assert past % NSA_TK == 0 and T <= TQ and past % SEL_BLOCK == 0 and T <= SEL_BLOCK
    assert (last_q - (CMP_BLOCK - 1)) // CMP_STRIDE * CMP_STRIDE + CMP_BLOCK <= past
    assert win_len == WINDOW and past >= win_len
    nsb = past // SEL_BLOCK + 1
    n_in = proj.shape[1]
    cpp = page // CMP_STRIDE
    xcmp = cache_kv[page_table, :, 0:2].reshape(B, n_pages, cpp, CMP_STRIDE, 2, G, dh)
    xcmp = xcmp.transpose(4, 0, 5, 1, 2, 3, 6).reshape(2, B * G, n_cmp, CMP_STRIDE * dh)
    kvc = nsa_compress_blocks(xcmp, pe, w1, w2, F32).reshape(2, B, G, n_cmp, dh)
    ks = cache_kv[page_table, :, 2].transpose(0, 3, 1, 2, 4).reshape(B, G, past, dh)
    vs = cache_kv[page_table, :, 3].transpose(0, 3, 1, 2, 4).reshape(B, G, past, dh)
    kvn = proj[:, NSA_Q_DIM:NSA_Q_DIM + NSA_KV_DIM].reshape(B, T, 6, G, dh)

    def rows_pad(a, n):
        a = a.transpose(0, 2, 1, 3)
        return jnp.pad(a, ((0, 0), (0, 0), (0, n - a.shape[2]), (0, 0)))

    kt, vt = rows_pad(kvn[:, :, 2], SEL_BLOCK), rows_pad(kvn[:, :, 3], SEL_BLOCK)
    win = jnp.concatenate([cache_win, kvn[:, :, 4:]], axis=1)
    win_keys = _round_up(win_len + T, 8)
    kw, vw = rows_pad(win[:, :, 0], win_keys), rows_pad(win[:, :, 1], win_keys)
    q5 = (proj[:, :NSA_Q_DIM] * (dh ** -0.5)).reshape(B, T, G, H, dh).transpose(0, 2, 3, 1, 4)
    q5 = jnp.pad(q5, ((0, 0), (0, 0), (0, 0), (0, TQ - T), (0, 0)))
    gate_col = NSA_Q_DIM + NSA_KV_DIM
    gates = jnp.pad(proj[:, gate_col:gate_col + LANE].reshape(B, T, LANE), ((0, 0), (0, TQ - T), (0, 0)))
    slopes, cmap, expand, nsb_pad = _nsa_consts(n_cmp, nsb, past // NSA_TK)
    per_bg = lambda n: pl.BlockSpec((1, 1, n, dh), lambda b, g, i: (b, g, 0, 0))
    kern = functools.partial(_nsa_attn_kernel, tq=TQ, q_start=past, n_past_tiles=past // NSA_TK, n_cmp=n_cmp,
                             nsb=nsb_pad, win_keys=win_keys, win_len=win_len, win_start=past - win_len,
                             tail_start=past)
    o5 = pl.pallas_call(
        kern,
        grid=(B, G, 1),
        in_specs=[pl.BlockSpec((1, 1, H, TQ, dh), lambda b, g, i: (b, g, 0, 0, 0)),
                  per_bg(n_cmp), per_bg(n_cmp), per_bg(past), per_bg(past), per_bg(win_keys), per_bg(win_keys),
                  pl.BlockSpec((1, TQ, LANE), lambda b, g, i: (b, 0, 0)),
                  pl.BlockSpec((1, H, 1, LANE), lambda b, g, i: (g, 0, 0, 0)),
                  pl.BlockSpec(cmap.shape, lambda b, g, i: (0, 0)),
                  pl.BlockSpec(expand.shape, lambda b, g, i: (0, 0, 0)),
                  per_bg(SEL_BLOCK), per_bg(SEL_BLOCK)],
        out_specs=pl.BlockSpec((1, 1, H, TQ, dh), lambda b, g, i: (b, g, 0, 0, 0)),
        out_shape=jax.ShapeDtypeStruct((B, G, H, TQ, dh), F32),
        compiler_params=_cparams("arbitrary", "arbitrary", "arbitrary"),
        name="nsa_attention_decode",
    )(q5, kvc[0], kvc[1], ks, vs, kw, vw, gates, slopes, cmap, expand, kt, vt)
    o = o5[:, :, :, :T].transpose(0, 3, 1, 2, 4).reshape(B * T, G * H * dh)
    return o, win[:, win.shape[1] - win_len:]


def causal_dwconv(xp, w):
    return lax.conv_general_dilated(xp, w[:, None, :], window_strides=(1,), padding='VALID',
                                    dimension_numbers=('NWC', 'WIO', 'NWC'),
                                    feature_group_count=xp.shape[-1])


def l2norm(x):
    return x * lax.rsqrt(jnp.sum(x * x, axis=-1, keepdims=True) + 1e-6)


def gated_delta_chunked(q, k, v, g, beta, s0, chunk):
    B, T, H, DK = q.shape
    DV = v.shape[-1]
    n = T // chunk

    def blk(a):
        return jnp.moveaxis(a.reshape(B, n, chunk, H, *a.shape[3:]), 3, 2)

    q, k, v, g, beta = blk(q), blk(k), blk(v), blk(g), blk(beta)
    gc = jnp.cumsum(g, axis=-1)
    causal = jnp.tril(jnp.ones((chunk, chunk), dtype=bool))
    strict = jnp.tril(jnp.ones((chunk, chunk), dtype=bool), -1)
    diff = gc[..., :, None] - gc[..., None, :]
    decay = jnp.where(causal, jnp.exp(jnp.where(causal, diff, 0.0)), 0.0)
    kk = jnp.einsum('bnhik,bnhjk->bnhij', k, k)
    lower = jnp.where(strict, beta[..., :, None] * kk * decay, 0.0)
    a_mat = lower + jnp.eye(chunk, dtype=lower.dtype)
    rhs = jnp.concatenate([v * beta[..., None], k * (beta * jnp.exp(gc))[..., None]], axis=-1)
    sol = lax.linalg.triangular_solve(a_mat, rhs, left_side=True, lower=True, unit_diagonal=True)
    u, w = sol[..., :DV], sol[..., DV:]
    qk = jnp.einsum('bnhik,bnhjk->bnhij', q, k) * decay
    q_dec = q * jnp.exp(gc)[..., None]
    k_dec = k * jnp.exp(gc[..., -1:] - gc)[..., None]
    g_last = jnp.exp(gc[..., -1])

    def step(s, xs):
        u_c, w_c, qk_c, qd_c, kd_c, gl_c = xs
        v_new = u_c - jnp.einsum('bhck,bhkv->bhcv', w_c, s)
        o_c = jnp.einsum('bhck,bhkv->bhcv', qd_c, s) + jnp.einsum('bhij,bhjv->bhiv', qk_c, v_new)
        s = s * gl_c[..., None, None] + jnp.einsum('bhck,bhcv->bhkv', kd_c, v_new)
        return s, o_c

    xs = tuple(jnp.moveaxis(a, 1, 0) for a in (u, w, qk, q_dec, k_dec, g_last))
    s, o = lax.scan(step, s0, xs)
    o = jnp.moveaxis(jnp.moveaxis(o, 0, 1), 2, 3).reshape(B, T, H, DV)
    return o, s


def gdn_core(proj, B, T, conv_buf, s0, conv_w, a_log, dt_bias, norm_w, chunk):
    qkv_raw = proj[..., :3 * GDN_DIM]
    z = proj[..., 3 * GDN_DIM:4 * GDN_DIM].reshape(B, T, GDN_HEADS, GDN_DV)
    a = proj[..., 4 * GDN_DIM:4 * GDN_DIM + GDN_HEADS]
    b = proj[..., 4 * GDN_DIM + GDN_HEADS:4 * GDN_DIM + 2 * GDN_HEADS]
    xpad = jnp.concatenate([conv_buf, qkv_raw], axis=1)
    qkv = jax.nn.silu(causal_dwconv(xpad, conv_w))
    q, k, v = jnp.split(qkv, 3, axis=-1)
    q = l2norm(q.reshape(B, T, GDN_HEADS, GDN_DK)) * (GDN_DK ** -0.5)
    k = l2norm(k.reshape(B, T, GDN_HEADS, GDN_DK))
    v = v.reshape(B, T, GDN_HEADS, GDN_DV)
    beta = jax.nn.sigmoid(b)
    g = -jnp.exp(a_log) * jax.nn.softplus(a + dt_bias)
    o, s = gated_delta_chunked(q, k, v, g, beta, s0, chunk)
    of = o * lax.rsqrt(jnp.mean(o * o, axis=-1, keepdims=True) + RMS_EPS) * norm_w
    o = of * jax.nn.silu(z)
    return o.reshape(B, T, GDN_HEADS * GDN_DV), xpad[:, xpad.shape[1] - (CONV_WIDTH - 1):], s


def complex_linear_combine(e1, e2):
    a1r, a1i, b1r, b1i = e1
    a2r, a2i, b2r, b2i = e2
    return (a2r * a1r - a2i * a1i, a2r * a1i + a2i * a1r,
            a2r * b1r - a2i * b1i + b2r, a2r * b1i + a2i * b1r + b2i)


def s5_core(u, B, T, h0r, h0i, lam_re, lam_im, b_re, b_im, c_re, c_im, d_skip, log_dt, w_glu):
    ug = u.reshape(B, T, S5_GROUPS, S5_GROUP)
    dt = jnp.exp(log_dt)[:, None]
    lr, li = lam_re, lam_im
    mag = jnp.exp(lr * dt)
    ar, ai = mag * jnp.cos(li * dt), mag * jnp.sin(li * dt)
    den = lr * lr + li * li
    fr = ((ar - 1.0) * lr + ai * li) / den
    fi = (ai * lr - (ar - 1.0) * li) / den
    bbar_re = fr[..., None] * b_re - fi[..., None] * b_im
    bbar_im = fr[..., None] * b_im + fi[..., None] * b_re
    bu_re = jnp.einsum('gpc,btgc->btgp', bbar_re, ug)
    bu_im = jnp.einsum('gpc,btgc->btgp', bbar_im, ug)
    bu_re = bu_re.at[:, 0].add(ar * h0r - ai * h0i)
    bu_im = bu_im.at[:, 0].add(ar * h0i + ai * h0r)
    a_re = jnp.broadcast_to(ar, bu_re.shape)
    a_im = jnp.broadcast_to(ai, bu_im.shape)
    _, _, hr, hi = lax.associative_scan(complex_linear_combine, (a_re, a_im, bu_re, bu_im), axis=1)
    y = jnp.einsum('gcp,btgp->btgc', c_re, hr) - jnp.einsum('gcp,btgp->btgc', c_im, hi)
    y = y.reshape(B, T, D_MODEL) + d_skip * u.reshape(B, T, D_MODEL)
    z = jax.nn.gelu(y)
    z = z * jax.nn.sigmoid(z @ w_glu)
    return z, hr[:, -1], hi[:, -1]


GDN_CB = 8


def _gdn_kernel(q_ref, k_ref, v_ref, z_ref, wq_ref, wk_ref, wv_ref, gcol_ref, bcol_ref, grow_ref, nw_ref,
                o_ref, sout_ref, s_ref, tq_ref, tk_ref, tv_ref, *, n_chunks, chunk, mdt):
    C, CB = chunk, n_chunks
    rows = C * CB
    i = pl.program_id(2)

    @pl.when(i == 0)
    def _():
        s_ref[...] = jnp.zeros_like(s_ref)
        tq_ref[...] = jnp.zeros_like(tq_ref)
        tk_ref[...] = jnp.zeros_like(tk_ref)
        tv_ref[...] = jnp.zeros_like(tv_ref)

    def conv_act(x_ref, w_ref, tail_ref):
        x = x_ref[...]
        w = w_ref[...]
        xx = jnp.concatenate([tail_ref[...], x], axis=0)
        y = w[CONV_WIDTH - 1:CONV_WIDTH] * x
        for j in range(CONV_WIDTH - 1):
            off = 8 - (CONV_WIDTH - 1) + j
            y = y + w[j:j + 1] * xx[off:off + rows]
        tail_ref[...] = x[rows - 8:rows]
        return y * (1.0 / (1.0 + jnp.exp(-y)))

    def l2n(x):
        return x * lax.rsqrt(jnp.sum(x * x, axis=-1, keepdims=True) + 1e-6)

    def mm(a, b):
        return _dot3(a.astype(mdt), b.astype(mdt), ((1,), (0,)))

    def mm_nt(a, b):
        return _dot3(a.astype(mdt), b.astype(mdt), ((1,), (1,)))

    def mm3(a, b):
        return _dot3(a, b, ((1,), (0,)))

    q = l2n(conv_act(q_ref, wq_ref, tq_ref)) * (GDN_DK ** -0.5)
    k = l2n(conv_act(k_ref, wk_ref, tk_ref))
    v = conv_act(v_ref, wv_ref, tv_ref)

    ii = lax.broadcasted_iota(jnp.int32, (C, C), 0)
    jj = lax.broadcasted_iota(jnp.int32, (C, C), 1)
    causal = jj <= ii
    strict = jj < ii
    eye = jnp.where(ii == jj, 1.0, 0.0)

    prep = []
    for c in range(CB):
        r = slice(c * C, (c + 1) * C)
        qc, kc, vc = q[r], k[r], v[r]
        gcl = jnp.broadcast_to(gcol_ref[0, 0, r, :], (C, GDN_DV))
        bcl = jnp.broadcast_to(bcol_ref[0, 0, r, :], (C, GDN_DV))
        gr = jnp.broadcast_to(grow_ref[0, 0, c:c + 1, :], (C, C))
        dec = jnp.where(causal, jnp.exp(jnp.where(causal, gcl[:, 0:C] - gr, 0.0)), 0.0)
        n = jnp.where(strict, bcl[:, 0:C] * mm_nt(kc, kc) * dec, 0.0)
        t = eye - n
        p = n
        width = 1
        while 2 * width < C:
            p = mm3(p, p)
            t = t + mm3(t, p)
            width *= 2
        eg = jnp.exp(gcl)
        sol = mm3(t, jnp.concatenate([vc * bcl, kc * (bcl * eg)], axis=1))
        gl = gcl[C - 1:C, :]
        prep.append((sol[:, 0:GDN_DV], sol[:, GDN_DV:], mm_nt(qc, kc) * dec, qc * eg,
                     kc * jnp.exp(gl - gcl), jnp.exp(gl)))

    s = s_ref[...]
    outs = []
    for c in range(CB):
        u, w, qk, qd, kd, egl = prep[c]
        vn = u - mm(w, s)
        outs.append(mm(qd, s) + mm(qk, vn))
        s = s * egl + mm(kd.T, vn)
    s_ref[...] = s
    sout_ref[0, 0] = s

    o = jnp.concatenate(outs, axis=0)
    o = o * lax.rsqrt(jnp.mean(o * o, axis=-1, keepdims=True) + RMS_EPS) * nw_ref[...]
    zz = z_ref[...]
    o_ref[...] = (o * (zz * (1.0 / (1.0 + jnp.exp(-zz))))).astype(o_ref.dtype)


def gdn_prompt(proj, B, T, conv_w, a_log, dt_bias, norm_w, mdt=BF16):
    H, C, CB = GDN_HEADS, GDN_CHUNK, GDN_CB
    assert GDN_DK == LANE and GDN_DV == LANE and T % (C * CB) == 0
    n = T // C
    nblk = n // CB
    rows = C * CB
    a = proj[:, 4 * GDN_DIM:4 * GDN_DIM + H]
    b = proj[:, 4 * GDN_DIM + H:4 * GDN_DIM + 2 * H]
    beta = jax.nn.sigmoid(b)
    g = -jnp.exp(a_log) * jax.nn.softplus(a + dt_bias)
    gc = jnp.cumsum(g.reshape(B, n, C, H), axis=2).transpose(0, 3, 1, 2)
    gcol = gc.reshape(B, H, T, 1)
    bcol = beta.reshape(B, T, H).transpose(0, 2, 1).reshape(B, H, T, 1)
    hb = GDN_DIM // LANE
    col = lambda k: pl.BlockSpec((rows, LANE), lambda b_, h, i: (b_ * nblk + i, k * hb + h))
    wcol = lambda k: pl.BlockSpec((CONV_WIDTH, LANE), lambda b_, h, i: (0, k * hb + h))
    vec = pl.BlockSpec((1, 1, rows, 1), lambda b_, h, i: (b_, h, i, 0))
    kern = functools.partial(_gdn_kernel, n_chunks=CB, chunk=C, mdt=mdt)
    o, s = pl.pallas_call(
        kern,
        grid=(B, H, nblk),
        in_specs=[col(0), col(1), col(2), col(3), wcol(0), wcol(1), wcol(2), vec, vec,
                  pl.BlockSpec((1, 1, CB, C), lambda b_, h, i: (b_, h, i, 0)),
                  pl.BlockSpec((1, LANE), lambda b_, h, i: (0, 0))],
        out_specs=[pl.BlockSpec((rows, LANE), lambda b_, h, i: (b_ * nblk + i, h)),
                   pl.BlockSpec((1, 1, GDN_DK, GDN_DV), lambda b_, h, i: (b_, h, 0, 0))],
        out_shape=[jax.ShapeDtypeStruct((B * T, H * GDN_DV), mdt),
                   jax.ShapeDtypeStruct((B, H, GDN_DK, GDN_DV), F32)],
        scratch_shapes=[pltpu.VMEM((GDN_DK, GDN_DV), F32)] + [pltpu.VMEM((8, LANE), F32)] * 3,
        compiler_params=_cparams("arbitrary", "arbitrary", "arbitrary"),
        name="gdn_prefill",
    )(proj, proj, proj, proj, conv_w, conv_w, conv_w, gcol, bcol, gc, norm_w.reshape(1, LANE))
    cv = proj[:, :3 * GDN_DIM].reshape(B, T, 3 * GDN_DIM)[:, T - (CONV_WIDTH - 1):]
    return o, s, cv


S5_LANES = S5_GROUPS * S5_STATE
S5_SETS = D_MODEL // LANE
S5_SET_LANES = S5_LANES // S5_SETS


def _gelu_tanh(x):
    return x * (0.5 * (1.0 + jnp.tanh(math.sqrt(2.0 / math.pi) * (x + 0.044715 * (x * x * x)))))


def _s5_kernel(u_ref, x_ref, g_ref, h0r_ref, h0i_ref, ar_ref, ai_ref, bb_ref, cre_ref, cim_ref, d_ref,
               wglu_ref, wout_ref, o_ref, fr_ref, fi_ref, hr_ref, hi_ref, cr_ref, ci_ref, pwr_ref, pwi_ref,
               *, n_streams, n_steps, chain, chunk_lanes):
    S, L, CW = n_streams, n_steps, chunk_lanes
    i = pl.program_id(1)

    @pl.when(i == 0)
    def _():
        pr, pi = ar_ref[...], ai_ref[...]
        a_r, a_i = pr, pi
        pwr_ref[0:1, :] = pr
        pwi_ref[0:1, :] = pi
        for l in range(1, L):
            pr, pi = pr * a_r - pi * a_i, pr * a_i + pi * a_r
            pwr_ref[l:l + 1, :] = pr
            pwi_ref[l:l + 1, :] = pi
        if chain:
            cr_ref[...] = h0r_ref[0]
            ci_ref[...] = h0i_ref[0]

    nb = CW // LANE

    def load_rows(ref, l, ch):
        parts = [ref[ch * nb + q, pl.ds(l, S, stride=L), :] for q in range(nb)]
        return parts[0] if nb == 1 else jnp.concatenate(parts, axis=1)

    def store_rows(ref, l, ch, val):
        for q in range(nb):
            ref[ch * nb + q, pl.ds(l, S, stride=L), :] = val[:, q * LANE:(q + 1) * LANE]

    bps = S5_SET_LANES // LANE
    for s in range(S5_SETS):
        res = _dot(u_ref[:, s * LANE:(s + 1) * LANE], bb_ref[s])
        for q in range(bps):
            hr_ref[s * bps + q] = res[:, q * LANE:(q + 1) * LANE]
            hi_ref[s * bps + q] = res[:, S5_SET_LANES + q * LANE:S5_SET_LANES + (q + 1) * LANE]

    for ch in range(S5_LANES // CW):
        lanes = slice(ch * CW, (ch + 1) * CW)
        a_r = jnp.broadcast_to(ar_ref[:, lanes], (S, CW))
        a_i = jnp.broadcast_to(ai_ref[:, lanes], (S, CW))
        if chain:
            init = (jnp.zeros((S, CW), F32), jnp.zeros((S, CW), F32))
        else:
            init = (h0r_ref[0, :, lanes], h0i_ref[0, :, lanes])

        def scan_body(l, carry, ch=ch, a_r=a_r, a_i=a_i):
            h_r, h_i = carry
            n_r = a_r * h_r - a_i * h_i + load_rows(hr_ref, l, ch)
            n_i = a_r * h_i + a_i * h_r + load_rows(hi_ref, l, ch)
            store_rows(hr_ref, l, ch, n_r)
            store_rows(hi_ref, l, ch, n_i)
            return n_r, n_i

        e_r, e_i = lax.fori_loop(0, L, scan_body, init, unroll=min(L, 8))
        if chain:
            al_r, al_i = pwr_ref[L - 1:L, lanes], pwi_ref[L - 1:L, lanes]
            s_r, s_i = cr_ref[:, lanes], ci_ref[:, lanes]
            before_r, before_i = [], []
            for c in range(S):
                before_r.append(s_r)
                before_i.append(s_i)
                s_r, s_i = (e_r[c:c + 1] + al_r * s_r - al_i * s_i, e_i[c:c + 1] + al_r * s_i + al_i * s_r)
            cr_ref[:, lanes] = s_r
            ci_ref[:, lanes] = s_i
            b_r = jnp.concatenate(before_r, axis=0)
            b_i = jnp.concatenate(before_i, axis=0)

            def fix_body(l, carry, ch=ch, lanes=lanes, b_r=b_r, b_i=b_i):
                p_r = pwr_ref[pl.ds(l, 1), lanes]
                p_i = pwi_ref[pl.ds(l, 1), lanes]
                store_rows(hr_ref, l, ch, load_rows(hr_ref, l, ch) + (p_r * b_r - p_i * b_i))
                store_rows(hi_ref, l, ch, load_rows(hi_ref, l, ch) + (p_r * b_i + p_i * b_r))
                return carry

            lax.fori_loop(0, L, fix_body, 0, unroll=min(L, 8))
        else:
            fr_ref[0, :, lanes] = e_r
            fi_ref[0, :, lanes] = e_i

    if chain:
        fr_ref[0] = cr_ref[...]
        fi_ref[0] = ci_ref[...]

    ys = []
    for s in range(S5_SETS):
        h_r = jnp.concatenate([hr_ref[s * bps + q] for q in range(bps)], axis=1)
        h_i = jnp.concatenate([hi_ref[s * bps + q] for q in range(bps)], axis=1)
        ys.append(_dot(h_r, cre_ref[s]) - _dot(h_i, cim_ref[s]))
    y = jnp.concatenate(ys, axis=1) + d_ref[...] * u_ref[...]
    z = _gelu_tanh(y)
    gate = _dot(z, wglu_ref[...])
    z = z * (1.0 / (1.0 + jnp.exp(-gate)))
    o_ref[...] = x_ref[...] + g_ref[0] * _dot(z, wout_ref[...])


def s5_discretize(lam_re, lam_im, b_re, b_im, c_re, c_im, log_dt, wdtype):
    dt = jnp.exp(log_dt)[:, None]
    mag = jnp.exp(lam_re * dt)
    ar, ai = mag * jnp.cos(lam_im * dt), mag * jnp.sin(lam_im * dt)
    den = lam_re * lam_re + lam_im * lam_im
    fr = ((ar - 1.0) * lam_re + ai * lam_im) / den
    fi = (ai * lam_re - (ar - 1.0) * lam_im) / den
    bbar_re = fr[..., None] * b_re - fi[..., None] * b_im
    bbar_im = fr[..., None] * b_im + fi[..., None] * b_re
    gps = S5_GROUPS // S5_SETS
    eye = jnp.eye(gps, dtype=F32)

    def in_blocks(bbar):
        bb = bbar.reshape(S5_SETS, gps, S5_STATE, S5_GROUP)
        return jnp.einsum('ab,sapc->sacbp', eye, bb).reshape(S5_SETS, LANE, S5_SET_LANES)

    def out_blocks(c):
        cc = c.reshape(S5_SETS, gps, S5_GROUP, S5_STATE)
        return jnp.einsum('ab,sacp->sapbc', eye, cc).reshape(S5_SETS, S5_SET_LANES, LANE)

    bb = jnp.concatenate([in_blocks(bbar_re), in_blocks(bbar_im)], axis=-1).astype(wdtype)
    return (ar.reshape(1, S5_LANES), ai.reshape(1, S5_LANES), bb,
            out_blocks(c_re).astype(wdtype), out_blocks(c_im).astype(wdtype))


def s5_mixer(u, x, g, h0r, h0i, disc, d_skip, w_glu, w_out, *, n_streams, n_steps, chain, chunk_lanes,
             rows_per_batch):
    m, d = u.shape
    tm = n_streams * n_steps
    ar, ai, bb, cre, cim = disc
    nb = h0r.shape[0]
    tiles_per_batch = m // tm // nb
    s0 = h0r.shape[1]
    const2 = lambda b, i: (0, 0)
    const3 = lambda b, i: (0, 0, 0)
    row = lambda b, i: (b * tiles_per_batch + i, 0)
    kern = functools.partial(_s5_kernel, n_streams=n_streams, n_steps=n_steps, chain=chain,
                             chunk_lanes=chunk_lanes)
    return pl.pallas_call(
        kern,
        grid=(nb, tiles_per_batch),
        in_specs=[pl.BlockSpec((tm, d), row),
                  pl.BlockSpec((tm, d), row),
                  pl.BlockSpec((1, 1, d), lambda b, i: (b, 0, 0)) if g.shape[1] == 1 else
                  pl.BlockSpec((1, tm, d), lambda b, i: (0, b * tiles_per_batch + i, 0)),
                  pl.BlockSpec((1, s0, S5_LANES), lambda b, i: (b, 0, 0)),
                  pl.BlockSpec((1, s0, S5_LANES), lambda b, i: (b, 0, 0)),
                  pl.BlockSpec((1, S5_LANES), const2),
                  pl.BlockSpec((1, S5_LANES), const2),
                  pl.BlockSpec(bb.shape, const3),
                  pl.BlockSpec(cre.shape, const3),
                  pl.BlockSpec(cim.shape, const3),
                  pl.BlockSpec((1, d), const2),
                  pl.BlockSpec((d, d), const2),
                  pl.BlockSpec((d, d), const2)],
        out_specs=[pl.BlockSpec((tm, d), row),
                   pl.BlockSpec((1, s0, S5_LANES), lambda b, i: (b, 0, 0)),
                   pl.BlockSpec((1, s0, S5_LANES), lambda b, i: (b, 0, 0))],
        out_shape=[jax.ShapeDtypeStruct((m, d), F32),
                   jax.ShapeDtypeStruct(h0r.shape, F32),
                   jax.ShapeDtypeStruct(h0r.shape, F32)],
        scratch_shapes=[pltpu.VMEM((S5_LANES // LANE, tm, LANE), F32),
                        pltpu.VMEM((S5_LANES // LANE, tm, LANE), F32),
                        pltpu.VMEM((1, S5_LANES), F32), pltpu.VMEM((1, S5_LANES), F32),
                        pltpu.VMEM((n_steps, S5_LANES), F32), pltpu.VMEM((n_steps, S5_LANES), F32)],
        compiler_params=_cparams("arbitrary", "arbitrary"),
        name="s5_mixer",
    )(u, x, g, h0r, h0i, ar, ai, bb, cre, cim, d_skip.reshape(1, d), w_glu, w_out)


def _pad_cols(w, n):
    return jnp.pad(w, ((0, 0), (0, n - w.shape[1])))


def kernel(x_prompt, x_sample, cache_nsa_kv, cache_nsa_win, state_gdn_s, state_gdn_conv, state_s5_re, state_s5_im, page_table, c_prompt, c_sample, ada_w, ada_b, norm_mix, norm_ffn, norm_final, nsa_w_in, nsa_cmp_pe, nsa_cmp_w1, nsa_cmp_w2, nsa_w_out, gdn_w_in, gdn_conv_w, gdn_a_log, gdn_dt_bias, gdn_norm, gdn_w_out, s5_w_in, s5_lambda_re, s5_lambda_im, s5_b_re, s5_b_im, s5_c_re, s5_c_im, s5_d, s5_log_dt, s5_w_glu, s5_w_out, ffn_w_gu, ffn_w_down, moe_router, moe_w_gu, moe_w_down):
    Bp, Tp, d = x_prompt.shape
    Bs, Ts, _ = x_sample.shape
    Mp, Ms = Bp * Tp, Bs * Ts
    xp = x_prompt.reshape(Mp, d)
    xs = x_sample.reshape(Ms, d)

    c_all = jnp.concatenate([c_prompt, c_sample], axis=0)
    r_pad = _round_up(c_all.shape[0], 8)
    mods = adaln_all(jnp.pad(c_all, ((0, r_pad - c_all.shape[0]), (0, 0))), ada_w, ada_b)

    def mods_of(i):
        parts = jnp.split(mods[i], 6, axis=-1)
        mp = [p[:Bp].reshape(Bp, 1, d) for p in parts]
        ms = [jnp.repeat(p[Bp:Bp + Bs], Ts, axis=0).reshape(1, Ms, d) for p in parts]
        return mp, ms

    P = dict(rows_per_batch=Tp)
    S = dict(rows_per_batch=Ts)

    nsa_kv_p, nsa_kv_s, nsa_win_p, nsa_win_s = [], [], [], []
    gdn_s_p, gdn_s_s, gdn_conv_p, gdn_conv_s = [], [], [], []
    s5_re_p, s5_re_s, s5_im_p, s5_im_s = [], [], [], []

    HI = "highest"
    for i in range(DEPTH):
        (sh1_p, sc1_p, g1_p, sh2_p, sc2_p, g2_p), (sh1_s, sc1_s, g1_s, sh2_s, sc2_s, g2_s) = mods_of(i)
        j = i // N_MIXERS
        if i % N_MIXERS == 0:
            n_in = _round_up(nsa_w_in.shape[2], LANE)
            w_in = _pad_cols(nsa_w_in[j], n_in)
            w_out = nsa_w_out[j]
            proj_p = ln_matmul(xp, norm_mix[i], sc1_p, sh1_p, w_in.astype(BF16), tm=512, tn=n_in, **P)
            proj_s = ln_matmul(xs, norm_mix[i], sc1_s, sh1_s, w_in, tm=Ms, tn=n_in // 3, **S)
            kv6 = proj_p[:, NSA_Q_DIM:NSA_Q_DIM + NSA_KV_DIM].reshape(Bp, Tp, 6, NSA_KV_HEADS, HEAD_DIM)
            cmp_w = (nsa_cmp_pe[j], nsa_cmp_w1[j], nsa_cmp_w2[j])
            o_p = nsa_prompt_attention(proj_p, Bp, Tp, *cmp_w)
            kv_p, win_p = kv6[:, :, :4], kv6[:, Tp - min(WINDOW, Tp):, 4:]
            o_s, win_s = nsa_sample_attention(proj_s, Bs, Ts, cache_nsa_kv[j], cache_nsa_win[j], page_table, *cmp_w)
            kv_s = proj_s[:, NSA_Q_DIM:NSA_Q_DIM + NSA_KV_DIM].reshape(Bs, Ts, 6, NSA_KV_HEADS, HEAD_DIM)[:, :, :4]
            nsa_kv_p.append(kv_p); nsa_kv_s.append(kv_s)
            nsa_win_p.append(win_p); nsa_win_s.append(win_s)
        elif i % N_MIXERS == 1:
            n_in = _round_up(gdn_w_in.shape[2], LANE)
            w_in = _pad_cols(gdn_w_in[j], n_in)
            w_out = gdn_w_out[j]
            proj_p = ln_matmul(xp, norm_mix[i], sc1_p, sh1_p, w_in.astype(BF16), tm=256, tn=n_in, **P)
            proj_s = ln_matmul(xs, norm_mix[i], sc1_s, sh1_s, w_in, tm=Ms, tn=n_in // 3, **S)
            o_p, st_p, cv_p = gdn_prompt(proj_p, Bp, Tp, gdn_conv_w[j], gdn_a_log[j], gdn_dt_bias[j], gdn_norm[j])
            with jax.default_matmul_precision(HI):
                o_s, cv_s, st_s = gdn_core(proj_s.reshape(Bs, Ts, n_in), Bs, Ts, state_gdn_conv[j],
                                           state_gdn_s[j], gdn_conv_w[j], gdn_a_log[j], gdn_dt_bias[j],
                                           gdn_norm[j], Ts)
            gdn_s_p.append(st_p); gdn_s_s.append(st_s)
            gdn_conv_p.append(cv_p); gdn_conv_s.append(cv_s)
        else:
            w_in = s5_w_in[j]
            w_out = s5_w_out[j]
            u_p = ln_matmul(xp, norm_mix[i], sc1_p, sh1_p, w_in.astype(BF16), tm=512, tn=d, **P)
            u_s = ln_matmul(xs, norm_mix[i], sc1_s, sh1_s, w_in, tm=Ms, tn=d, **S)
            s5p = (s5_lambda_re[j], s5_lambda_im[j], s5_b_re[j], s5_b_im[j], s5_c_re[j], s5_c_im[j], s5_log_dt[j])
            h00 = jnp.zeros((Bp, 1, S5_LANES), F32)
            xp, re_p, im_p = s5_mixer(u_p, xp, g1_p, h00, h00, s5_discretize(*s5p, BF16), s5_d[j],
                                      s5_w_glu[j].astype(BF16), w_out.astype(BF16),
                                      n_streams=8, n_steps=32, chain=True, chunk_lanes=512, **P)
            xs, re_s, im_s = s5_mixer(u_s, xs, g1_s, state_s5_re[j].reshape(1, Bs, S5_LANES),
                                      state_s5_im[j].reshape(1, Bs, S5_LANES), s5_discretize(*s5p, F32),
                                      s5_d[j], s5_w_glu[j], w_out,
                                      n_streams=Bs, n_steps=Ts, chain=False, chunk_lanes=LANE, **S)
            st_shape = (-1, S5_GROUPS, S5_STATE)
            s5_re_p.append(re_p.reshape(st_shape)); s5_re_s.append(re_s.reshape(st_shape))
            s5_im_p.append(im_p.reshape(st_shape)); s5_im_s.append(im_s.reshape(st_shape))
        if i % N_MIXERS != 2:
            xp = matmul_residual(o_p.reshape(Mp, -1), w_out.astype(BF16), xp, g1_p, tm=512, **P)
            xs = matmul_residual(o_s.reshape(Ms, -1), w_out, xs, g1_s, tm=Ms, **S)

        f = i // 2
        if i % 2 == 0:
            w_gu, w_down = ffn_w_gu[f], ffn_w_down[f]
            xp = dense_ffn(xp, norm_ffn[i], sc2_p, sh2_p, g2_p, w_gu.astype(BF16), w_down.astype(BF16),
                           tm=1024, tf=256, **P)
            xs = dense_ffn(xs, norm_ffn[i], sc2_s, sh2_s, g2_s, w_gu, w_down, tm=Ms, tf=256, **S)
        else:
            w_gu, w_down = moe_w_gu[f], moe_w_down[f]
            w_gu_b, w_down_b = w_gu.astype(BF16), w_down.astype(BF16)
            xp = moe_layer(xp, norm_ffn[i], sc2_p, sh2_p, g2_p, moe_router[f], w_gu_b, w_down_b,
                           tm_ln=512, tm=512, tf=512, **P)
            last = i == DEPTH - 1
            xs = moe_layer(xs, norm_ffn[i], sc2_s, sh2_s, g2_s, moe_router[f],
                           w_gu_b if last else w_gu, w_down_b if last else w_down,
                           tm_ln=Ms, tm=128, tf=512, **S)

    def final_norm(x):
        return x * lax.rsqrt(jnp.mean(x * x, axis=-1, keepdims=True) + RMS_EPS) * norm_final

    y_prompt = final_norm(xp).reshape(Bp, Tp, d)
    y_sample = final_norm(xs).reshape(Bs, Ts, d)
    return (y_prompt, y_sample, jnp.stack(nsa_kv_p), jnp.stack(nsa_kv_s), jnp.stack(nsa_win_p),
            jnp.stack(nsa_win_s), jnp.stack(gdn_s_p), jnp.stack(gdn_s_s), jnp.stack(gdn_conv_p),
            jnp.stack(gdn_conv_s), jnp.stack(s5_re_p), jnp.stack(s5_re_s), jnp.stack(s5_im_p),
            jnp.stack(s5_im_s))
```

```python
import functools
import math

import jax
import jax.numpy as jnp
from jax import lax
from jax.experimental import pallas as pl
from jax.experimental.pallas import tpu as pltpu

F32 = jnp.float32
BF16 = jnp.bfloat16

D_MODEL = 1024
DEPTH = 4
PAGE_SIZE = 128
N_MIXERS = 3

NSA_HEADS = 16
HEAD_DIM = D_MODEL // NSA_HEADS
NSA_KV_HEADS = 2
NSA_HPG = NSA_HEADS // NSA_KV_HEADS
CMP_BLOCK = 32
CMP_STRIDE = 16
SEL_BLOCK = 64
SEL_TOPK = 16
WINDOW = 512
NSA_Q_BLOCK = 128
NSA_Q_DIM = NSA_HEADS * HEAD_DIM
NSA_KV_DIM = 6 * NSA_KV_HEADS * HEAD_DIM

GDN_HEADS = 8
GDN_DK = 128
GDN_DV = 128
GDN_DIM = GDN_HEADS * GDN_DK
CONV_WIDTH = 4
GDN_CHUNK = 64

S5_GROUP = 16
S5_GROUPS = D_MODEL // S5_GROUP
S5_STATE = 64

D_FF = 2816
N_EXPERTS = 8
TOP_K = 2
D_FF_EXPERT = 3584

RMS_EPS = 1e-6
NEG_INF = -1e30

LANE = 128
VMEM_LIMIT_BYTES = 56 * 1024 * 1024


def _cparams(*sem):
    return pltpu.CompilerParams(dimension_semantics=sem, vmem_limit_bytes=VMEM_LIMIT_BYTES)


def _round_up(n, m):
    return -(-n // m) * m


def _norm_mod(x, nw, sc, sh):
    y = x * lax.rsqrt(jnp.mean(x * x, axis=-1, keepdims=True) + RMS_EPS)
    return (y * nw) * (1.0 + sc) + sh


def _mod_spec(mod, tm, rows_per_batch):
    if mod.shape[1] == 1:
        return pl.BlockSpec((1, 1, mod.shape[2]), lambda i, *_: (i * tm // rows_per_batch, 0, 0))
    return pl.BlockSpec((1, tm, mod.shape[2]), lambda i, *_: (0, i, 0))


def _adaln_kernel(c_ref, w_ref, b_ref, o_ref):
    c = c_ref[...]
    s = c / (1.0 + jnp.exp(-c))
    o_ref[0] = jnp.dot(s, w_ref[0], preferred_element_type=F32, precision=lax.Precision.HIGHEST) + b_ref[0]


def adaln_all(c, ada_w, ada_b):
    r, d = c.shape
    n = ada_w.shape[2]
    tn = 1536
    return pl.pallas_call(
        _adaln_kernel,
        grid=(DEPTH, n // tn),
        in_specs=[pl.BlockSpec((r, d), lambda l, j: (0, 0)),
                  pl.BlockSpec((1, d, tn), lambda l, j: (l, 0, j)),
                  pl.BlockSpec((1, 1, tn), lambda l, j: (l, 0, j))],
        out_specs=pl.BlockSpec((1, r, tn), lambda l, j: (l, 0, j)),
        out_shape=jax.ShapeDtypeStruct((DEPTH, r, n), F32),
        compiler_params=_cparams("arbitrary", "arbitrary"),
        name="adaln",
    )(c, ada_w, ada_b.reshape(DEPTH, 1, n))


def _dot(a, b):
    prec = lax.Precision.HIGHEST if b.dtype == F32 else None
    return jnp.dot(a.astype(b.dtype), b, preferred_element_type=F32, precision=prec)


def _ln_mm_kernel(x_ref, nw_ref, sc_ref, sh_ref, w_ref, o_ref, h_ref):
    @pl.when(pl.program_id(1) == 0)
    def _():
        h_ref[...] = _norm_mod(x_ref[...], nw_ref[...], sc_ref[0], sh_ref[0]).astype(h_ref.dtype)

    o_ref[...] = _dot(h_ref[...], w_ref[...])


def ln_matmul(x, nw, sc, sh, w, *, tm, tn, rows_per_batch):
    m, d = x.shape
    n = w.shape[1]
    return pl.pallas_call(
        _ln_mm_kernel,
        grid=(m // tm, n // tn),
        in_specs=[pl.BlockSpec((tm, d), lambda i, j: (i, 0)),
                  pl.BlockSpec((1, d), lambda i, j: (0, 0)),
                  _mod_spec(sc, tm, rows_per_batch),
                  _mod_spec(sh, tm, rows_per_batch),
                  pl.BlockSpec((d, tn), lambda i, j: (0, j))],
        out_specs=pl.BlockSpec((tm, tn), lambda i, j: (i, j)),
        out_shape=jax.ShapeDtypeStruct((m, n), F32),
        scratch_shapes=[pltpu.VMEM((tm, d), w.dtype)],
        compiler_params=_cparams("arbitrary", "arbitrary"),
        name="ln_matmul",
    )(x, nw.reshape(1, d), sc, sh, w)


def _mm_res_kernel(a_ref, w_ref, x_ref, g_ref, o_ref):
    o_ref[...] = x_ref[...] + g_ref[0] * _dot(a_ref[...], w_ref[...])


def matmul_residual(a, w, x, g, *, tm, rows_per_batch):
    m, k = a.shape
    n = w.shape[1]
    return pl.pallas_call(
        _mm_res_kernel,
        grid=(m // tm,),
        in_specs=[pl.BlockSpec((tm, k), lambda i: (i, 0)),
                  pl.BlockSpec((k, n), lambda i: (0, 0)),
                  pl.BlockSpec((tm, n), lambda i: (i, 0)),
                  _mod_spec(g, tm, rows_per_batch)],
        out_specs=pl.BlockSpec((tm, n), lambda i: (i, 0)),
        out_shape=jax.ShapeDtypeStruct((m, n), F32),
        compiler_params=_cparams("arbitrary"),
        name="matmul_residual",
    )(a, w, x, g)


def _silu(a):
    return a / (1.0 + jnp.exp(-a))


def _ffn_kernel(x_ref, nw_ref, sc_ref, sh_ref, g_ref, wa_ref, wb_ref, wd_ref, o_ref, h_ref, acc_ref):
    j = pl.program_id(1)

    @pl.when(j == 0)
    def _():
        h_ref[...] = _norm_mod(x_ref[...], nw_ref[...], sc_ref[0], sh_ref[0]).astype(h_ref.dtype)
        acc_ref[...] = jnp.zeros_like(acc_ref)

    h = h_ref[...]
    a = _dot(h, wa_ref[...])
    b = _dot(h, wb_ref[...])
    acc_ref[...] += _dot(_silu(a) * b, wd_ref[...])

    @pl.when(j == pl.num_programs(1) - 1)
    def _():
        o_ref[...] = x_ref[...] + g_ref[0] * acc_ref[...]


def dense_ffn(x, nw, sc, sh, g, w_gu, w_down, *, tm, tf, rows_per_batch):
    m, d = x.shape
    ff = w_down.shape[0]
    nj = ff // tf
    return pl.pallas_call(
        _ffn_kernel,
        grid=(m // tm, nj),
        in_specs=[pl.BlockSpec((tm, d), lambda i, j: (i, 0)),
                  pl.BlockSpec((1, d), lambda i, j: (0, 0)),
                  _mod_spec(sc, tm, rows_per_batch),
                  _mod_spec(sh, tm, rows_per_batch),
                  _mod_spec(g, tm, rows_per_batch),
                  pl.BlockSpec((d, tf), lambda i, j: (0, j)),
                  pl.BlockSpec((d, tf), lambda i, j: (0, nj + j)),
                  pl.BlockSpec((tf, d), lambda i, j: (j, 0))],
        out_specs=pl.BlockSpec((tm, d), lambda i, j: (i, 0)),
        out_shape=jax.ShapeDtypeStruct((m, d), F32),
        scratch_shapes=[pltpu.VMEM((tm, d), w_gu.dtype), pltpu.VMEM((tm, d), F32)],
        compiler_params=_cparams("arbitrary", "arbitrary"),
        name="dense_ffn",
    )(x, nw.reshape(1, d), sc, sh, g, w_gu, w_gu, w_down)


def _ln_router_kernel(x_ref, nw_ref, sc_ref, sh_ref, wr_ref, h_ref, lg_ref):
    h = _norm_mod(x_ref[...], nw_ref[...], sc_ref[0], sh_ref[0])
    h_ref[...] = h.astype(h_ref.dtype)
    lg_ref[...] = jnp.dot(h, wr_ref[...], preferred_element_type=F32, precision=lax.Precision.HIGHEST)


def ln_router(x, nw, sc, sh, w_router, *, tm, rows_per_batch, h_dtype):
    m, d = x.shape
    wr = jnp.pad(w_router, ((0, 0), (0, LANE - N_EXPERTS)))
    h, lg = pl.pallas_call(
        _ln_router_kernel,
        grid=(m // tm,),
        in_specs=[pl.BlockSpec((tm, d), lambda i: (i, 0)),
                  pl.BlockSpec((1, d), lambda i: (0, 0)),
                  _mod_spec(sc, tm, rows_per_batch),
                  _mod_spec(sh, tm, rows_per_batch),
                  pl.BlockSpec((d, LANE), lambda i: (0, 0))],
        out_specs=[pl.BlockSpec((tm, d), lambda i: (i, 0)),
                   pl.BlockSpec((tm, LANE), lambda i: (i, 0))],
        out_shape=[jax.ShapeDtypeStruct((m, d), h_dtype), jax.ShapeDtypeStruct((m, LANE), F32)],
        compiler_params=_cparams("arbitrary"),
        name="ln_router",
    )(x, nw.reshape(1, d), sc, sh, wr)
    return h, lg[:, :N_EXPERTS]


def _moe_kernel(be_ref, nu_ref, r_ref, wa_ref, wb_ref, wd_ref, o_ref, acc_ref):
    i = pl.program_id(0)
    j = pl.program_id(1)

    @pl.when(i < nu_ref[0])
    def _():
        @pl.when(j == 0)
        def _():
            acc_ref[...] = jnp.zeros_like(acc_ref)

        r = r_ref[...]
        a = _dot(r, wa_ref[0])
        b = _dot(r, wb_ref[0])
        acc_ref[...] += _dot(_silu(a) * b, wd_ref[0])

        @pl.when(j == pl.num_programs(1) - 1)
        def _():
            o_ref[...] = acc_ref[...]


def moe_experts(rows, block_e, n_used, w_gu, w_down, *, tm, tf):
    m, d = rows.shape
    ff = w_down.shape[1]
    nj = ff // tf
    n_blocks = m // tm

    def blk(i, nu):
        return jnp.minimum(i, nu[0] - 1)

    def jj(i, j, nu):
        return jnp.where(i < nu[0], j, nj - 1)

    grid_spec = pltpu.PrefetchScalarGridSpec(
        num_scalar_prefetch=2,
        grid=(n_blocks, nj),
        in_specs=[pl.BlockSpec((tm, d), lambda i, j, be, nu: (blk(i, nu), 0)),
                  pl.BlockSpec((1, d, tf), lambda i, j, be, nu: (be[blk(i, nu)], 0, jj(i, j, nu))),
                  pl.BlockSpec((1, d, tf), lambda i, j, be, nu: (be[blk(i, nu)], 0, nj + jj(i, j, nu))),
                  pl.BlockSpec((1, tf, d), lambda i, j, be, nu: (be[blk(i, nu)], jj(i, j, nu), 0))],
        out_specs=pl.BlockSpec((tm, d), lambda i, j, be, nu: (blk(i, nu), 0)),
        scratch_shapes=[pltpu.VMEM((tm, d), F32)],
    )
    return pl.pallas_call(
        _moe_kernel,
        grid_spec=grid_spec,
        out_shape=jax.ShapeDtypeStruct((m, d), F32),
        compiler_params=_cparams("arbitrary", "arbitrary"),
        name="moe_experts",
    )(block_e, n_used, rows, w_gu, w_gu, w_down)


def moe_layer(x, nw, sc, sh, g, w_router, w_gu, w_down, *, tm_ln, tm, tf, rows_per_batch):
    n_tok, d = x.shape
    n_rows = n_tok * TOP_K
    h, logits = ln_router(x, nw, sc, sh, w_router, tm=tm_ln, rows_per_batch=rows_per_batch, h_dtype=w_gu.dtype)
    top_logit, top_e = lax.top_k(logits, TOP_K)
    gate = jax.nn.softmax(top_logit, axis=-1)
    flat_e = top_e.reshape(-1)
    order = jnp.argsort(flat_e)
    e_sorted = flat_e[order]
    tok_sorted = order // TOP_K
    counts = jnp.bincount(flat_e, length=N_EXPERTS)
    padded = (counts + tm - 1) // tm * tm
    pad_end = jnp.cumsum(padded)
    pad_start = pad_end - padded
    grp_start = jnp.cumsum(counts) - counts
    dest = pad_start[e_sorted] + jnp.arange(n_rows) - grp_start[e_sorted]
    n_blocks = -(-n_rows // tm) + N_EXPERTS
    rows = jnp.zeros((n_blocks * tm, d), h.dtype).at[dest].set(h[tok_sorted])
    block_e = jnp.minimum(jnp.searchsorted(pad_end, jnp.arange(n_blocks) * tm, side='right'),
                          N_EXPERTS - 1).astype(jnp.int32)
    n_used = (pad_end[-1] // tm).astype(jnp.int32).reshape(1)
    out = moe_experts(rows, block_e, n_used, w_gu, w_down, tm=tm, tf=tf)
    pos = jnp.zeros((n_rows,), jnp.int32).at[order].set(dest.astype(jnp.int32)).reshape(n_tok, TOP_K)
    f = out[pos[:, 0]] * gate[:, 0:1] + out[pos[:, 1]] * gate[:, 1:2]
    if g.shape[1] == 1:
        gg = jnp.repeat(g[:, 0], rows_per_batch, axis=0)
    else:
        gg = g[0]
    return x + gg * f


def alibi_slopes():
    return jnp.exp2(-8.0 * (jnp.arange(NSA_HEADS, dtype=F32) + 1.0) / NSA_HEADS)


def nsa_split(proj, B, T):
    q = proj[..., :NSA_Q_DIM].reshape(B, T, NSA_HEADS, HEAD_DIM)
    kv = proj[..., NSA_Q_DIM:NSA_Q_DIM + NSA_KV_DIM].reshape(B, T, 6, NSA_KV_HEADS, HEAD_DIM)
    gates = jax.nn.sigmoid(proj[..., NSA_Q_DIM + NSA_KV_DIM:NSA_Q_DIM + NSA_KV_DIM + 3 * NSA_HEADS]
                           ).reshape(B, T, NSA_HEADS, 3)
    return q, kv, gates


def nsa_compress(seq, pe, w1, w2):
    B, L, G, dh = seq.shape
    r = CMP_BLOCK // CMP_STRIDE
    n_chunk = L // CMP_STRIDE
    nc = n_chunk - r + 1
    ch = seq.reshape(B, n_chunk, CMP_STRIDE, G, dh)
    blocks = jnp.concatenate([ch[:, j:j + nc] for j in range(r)], axis=2)
    blocks = blocks + pe[None, None, :, None, :]
    flat = blocks.transpose(0, 1, 3, 2, 4).reshape(B, nc, G, CMP_BLOCK * dh)
    return jax.nn.gelu(flat @ w1) @ w2


def cmp_to_sel_map(nc, nsb):
    c_start = jnp.arange(nc) * CMP_STRIDE
    s_start = jnp.arange(nsb) * SEL_BLOCK
    hit = (c_start[:, None] < s_start[None, :] + SEL_BLOCK) & (c_start[:, None] + CMP_BLOCK > s_start[None, :])
    return hit.astype(F32)


def nsa_attend_block(q, gates, q_pos, kc, vc, kc_pos, ks_t, vs_t, kw, vw, kw_pos):
    B, Tq = q.shape[:2]
    nc = kc.shape[1]
    nsb = ks_t.shape[2]
    n_sel = min(SEL_TOPK, nsb)
    slopes = alibi_slopes().reshape(NSA_KV_HEADS, NSA_HPG)
    qg = q.reshape(B, Tq, NSA_KV_HEADS, NSA_HPG, HEAD_DIM) * (HEAD_DIM ** -0.5)

    d_c = q_pos[:, None] - kc_pos[None, :]
    ok_c = (d_c >= 0)[None, :, None, None, :]
    s_c = jnp.einsum('btghd,bngd->btghn', qg, kc).astype(F32)
    s_c = s_c - slopes[None, None, :, :, None] * jnp.abs(d_c).astype(F32)[None, :, None, None, :]
    s_c = jnp.where(ok_c, s_c, NEG_INF)
    p_c = jax.nn.softmax(s_c, axis=-1) * ok_c
    o_c = jnp.einsum('btghn,bngd->btghd', p_c.astype(vc.dtype), vc)

    imp = jnp.einsum('btghn,nj->btgj', p_c, cmp_to_sel_map(nc, nsb))
    cur = q_pos // SEL_BLOCK
    blk = jnp.arange(nsb)
    is_cur = (blk[None, :] == cur[:, None])[None, :, None, :]
    is_past = (blk[None, :] < cur[:, None])[None, :, None, :]
    imp = jnp.where(is_cur, jnp.inf, jnp.where(is_past, imp, -jnp.inf))
    _, idx = lax.top_k(imp, n_sel)
    bi = jnp.arange(B)[:, None, None, None]
    gi = jnp.arange(NSA_KV_HEADS)[None, None, :, None]
    k_sel = ks_t[bi, gi, idx]
    v_sel = vs_t[bi, gi, idx]
    pos_s = idx[..., None] * SEL_BLOCK + jnp.arange(SEL_BLOCK)
    d_s = q_pos[None, :, None, None, None] - pos_s
    ok_s = (d_s >= 0)[:, :, :, None]
    s_s = jnp.einsum('btghd,btgnkd->btghnk', qg, k_sel).astype(F32)
    s_s = s_s - slopes[None, None, :, :, None, None] * jnp.abs(d_s).astype(F32)[:, :, :, None]
    s_s = jnp.where(ok_s, s_s, NEG_INF).reshape(B, Tq, NSA_KV_HEADS, NSA_HPG, n_sel * SEL_BLOCK)
    p_s = jax.nn.softmax(s_s, axis=-1).reshape(B, Tq, NSA_KV_HEADS, NSA_HPG, n_sel, SEL_BLOCK)
    o_s = jnp.einsum('btghnk,btgnkd->btghd', p_s.astype(v_sel.dtype), v_sel)

    d_w = q_pos[:, None] - kw_pos[None, :]
    ok_w = ((d_w >= 0) & (d_w < WINDOW) & (kw_pos >= 0)[None, :])[None, :, None, None, :]
    s_w = jnp.einsum('btghd,blgd->btghl', qg, kw).astype(F32)
    s_w = s_w - slopes[None, None, :, :, None] * jnp.abs(d_w).astype(F32)[None, :, None, None, :]
    p_w = jax.nn.softmax(jnp.where(ok_w, s_w, NEG_INF), axis=-1)
    o_w = jnp.einsum('btghl,blgd->btghd', p_w.astype(vw.dtype), vw)

    g = gates.reshape(B, Tq, NSA_KV_HEADS, NSA_HPG, 3).astype(q.dtype)
    o = g[..., 0:1] * o_c + g[..., 1:2] * o_s + g[..., 2:3] * o_w
    return o.reshape(B, Tq, NSA_Q_DIM)


def nsa_prompt_core(proj, B, T, pe, w1, w2):
    q, kv, gates = nsa_split(proj, B, T)
    kc = nsa_compress(kv[:, :, 0], pe[0], w1[0], w2[0])
    vc = nsa_compress(kv[:, :, 1], pe[1], w1[1], w2[1])
    kc_pos = jnp.arange(kc.shape[1]) * CMP_STRIDE + CMP_BLOCK - 1
    nsb = T // SEL_BLOCK
    ks_t = kv[:, :, 2].reshape(B, nsb, SEL_BLOCK, NSA_KV_HEADS, HEAD_DIM).transpose(0, 3, 1, 2, 4)
    vs_t = kv[:, :, 3].reshape(B, nsb, SEL_BLOCK, NSA_KV_HEADS, HEAD_DIM).transpose(0, 3, 1, 2, 4)
    win_pad = jnp.pad(kv[:, :, 4:], ((0, 0), (WINDOW, 0), (0, 0), (0, 0), (0, 0)))

    def q_block(i):
        start = i * NSA_Q_BLOCK
        qb = lax.dynamic_slice_in_dim(q, start, NSA_Q_BLOCK, axis=1)
        gb = lax.dynamic_slice_in_dim(gates, start, NSA_Q_BLOCK, axis=1)
        wb = lax.dynamic_slice_in_dim(win_pad, start, WINDOW + NSA_Q_BLOCK, axis=1)
        q_pos = start + jnp.arange(NSA_Q_BLOCK)
        kw_pos = start - WINDOW + jnp.arange(WINDOW + NSA_Q_BLOCK)
        return nsa_attend_block(qb, gb, q_pos, kc, vc, kc_pos, ks_t, vs_t, wb[:, :, 0], wb[:, :, 1], kw_pos)

    o = lax.map(q_block, jnp.arange(T // NSA_Q_BLOCK))
    o = o.transpose(1, 0, 2, 3).reshape(B, T, NSA_Q_DIM)
    return o, kv[:, :, :4], kv[:, T - min(WINDOW, T):, 4:]


def nsa_sample_core(proj, B, T, cache_kv, cache_win, page_table, pe, w1, w2):
    past_len = page_table.shape[1] * cache_kv.shape[1]
    win_len = cache_win.shape[1]
    q, kv, gates = nsa_split(proj, B, T)
    past = cache_kv[page_table].reshape(B, past_len, 4, NSA_KV_HEADS, HEAD_DIM)
    full = jnp.concatenate([past, kv[:, :, :4]], axis=1)
    L = past_len + T
    Lp = -(-L // SEL_BLOCK) * SEL_BLOCK
    full = jnp.pad(full, ((0, 0), (0, Lp - L), (0, 0), (0, 0), (0, 0)))
    kc = nsa_compress(full[:, :, 0], pe[0], w1[0], w2[0])
    vc = nsa_compress(full[:, :, 1], pe[1], w1[1], w2[1])
    kc_pos = jnp.arange(kc.shape[1]) * CMP_STRIDE + CMP_BLOCK - 1
    nsb = Lp // SEL_BLOCK
    ks_t = full[:, :, 2].reshape(B, nsb, SEL_BLOCK, NSA_KV_HEADS, HEAD_DIM).transpose(0, 3, 1, 2, 4)
    vs_t = full[:, :, 3].reshape(B, nsb, SEL_BLOCK, NSA_KV_HEADS, HEAD_DIM).transpose(0, 3, 1, 2, 4)
    win = jnp.concatenate([cache_win, kv[:, :, 4:]], axis=1)
    kw_pos = past_len - win_len + jnp.arange(win_len + T)
    q_pos = past_len + jnp.arange(T)
    o = nsa_attend_block(q, gates, q_pos, kc, vc, kc_pos, ks_t, vs_t, win[:, :, 0], win[:, :, 1], kw_pos)
    return o, kv[:, :, :4], win[:, win.shape[1] - win_len:]


NSA_TQ = 128
NSA_TK = 512


def _split3(x):
    hi = x.astype(BF16)
    return hi, (x - hi.astype(F32)).astype(BF16)


def _dot3(a, b, dims):
    dn = (dims, ((), ()))
    if b.dtype == BF16:
        return lax.dot_general(a.astype(BF16), b, dn, preferred_element_type=F32)
    a_hi, a_lo = _split3(a.astype(F32))
    b_hi, b_lo = _split3(b)
    return (lax.dot_general(a_hi, b_hi, dn, preferred_element_type=F32)
            + (lax.dot_general(a_hi, b_lo, dn, preferred_element_type=F32)
               + lax.dot_general(a_lo, b_hi, dn, preferred_element_type=F32)))


def _qk(q, k):
    return _dot3(q, k, ((1,), (1,)))


def _pv(p, v):
    return _dot3(p, v, ((1,), (0,)))


def _nsa_decode_kernel(q_ref, kc_ref, vc_ref, ks_ref, vs_ref, kw_ref, vw_ref, gate_ref, slope_ref, map_ref, e_ref,
                       kt_ref, vt_ref, o_ref, *, tq, q_start, n_cmp, nsb, win_keys, win_len, win_start):
    g = pl.program_id(1)
    TQ, TK, H = tq, NSA_TK, NSA_HPG
    R = H * TQ
    start = q_start
    n_pages, page = ks_ref.shape[1], ks_ref.shape[4]
    ppt = TK // page
    q = q_ref[0, 0].reshape(R, HEAD_DIM)
    slope_col = jnp.concatenate([jnp.broadcast_to(slope_ref[0, h][:, 0:1], (TQ, 1)) for h in range(H)], axis=0)
    qpos = start + lax.broadcasted_iota(jnp.int32, (TQ, 1), 0)
    qpos_r = jnp.concatenate([qpos] * H, axis=0)

    kcpos = lax.broadcasted_iota(jnp.int32, (1, n_cmp), 1) * CMP_STRIDE + (CMP_BLOCK - 1)
    ok_c = kcpos <= qpos_r
    s = _qk(q, kc_ref[0, 0]) + slope_col * (kcpos - start).astype(F32)
    s = jnp.where(ok_c, s, NEG_INF)
    e = jnp.exp(s - jnp.max(s, axis=-1, keepdims=True))
    pn = jnp.where(ok_c, e, 0.0) * (1.0 / jnp.sum(e, axis=-1, keepdims=True))
    o_c = _pv(pn, vc_ref[0, 0])
    p_grp = pn[0:TQ]
    for h in range(1, H):
        p_grp = p_grp + pn[h * TQ:(h + 1) * TQ]
    imp = jnp.dot(p_grp, map_ref[...], preferred_element_type=F32, precision=lax.Precision.HIGHEST)

    blk = lax.broadcasted_iota(jnp.int32, (1, nsb), 1)
    blk_f = blk.astype(F32)
    cur = qpos // SEL_BLOCK
    work = jnp.where(blk < cur, imp, -1.0)
    sel = jnp.where(blk == cur, 1.0, 0.0)
    for _ in range(min(SEL_TOPK, nsb) - 1):
        mx = jnp.max(work, axis=-1, keepdims=True)
        first = jnp.min(jnp.where(work == mx, blk_f, float(nsb)), axis=-1, keepdims=True)
        pick = jnp.logical_and(blk_f == first, mx >= 0.0)
        sel = jnp.where(pick, 1.0, sel)
        work = jnp.where(pick, -1.0, work)
    sel_b = sel.astype(BF16)

    def online_step(carry, s_raw, pv, kpos, mb):
        m, l, acc = carry
        s = s_raw + slope_col * (kpos - start).astype(F32) + jnp.concatenate([mb] * H, axis=0)
        m_new = jnp.maximum(m, jnp.max(s, axis=-1, keepdims=True))
        alpha = jnp.exp(m - m_new)
        p = jnp.exp(s - m_new)
        l = alpha * l + jnp.sum(p, axis=-1, keepdims=True)
        return m_new, l, alpha * acc + pv(p)

    def past_tile(j, carry):
        kpos = j * TK + lax.broadcasted_iota(jnp.int32, (1, TK), 1)
        mb = (jnp.dot(sel_b, e_ref[j], preferred_element_type=F32) - 1.0) * (-NEG_INF)
        s_raw = jnp.concatenate([_dot3(q, ks_ref[0, j * ppt + c, 0], ((1,), (0,))) for c in range(ppt)], axis=1)

        def pv(p):
            out = _dot3(p[:, 0:page], vs_ref[0, j * ppt, 0], ((1,), (1,)))
            for c in range(1, ppt):
                out = out + _dot3(p[:, c * page:(c + 1) * page], vs_ref[0, j * ppt + c, 0], ((1,), (1,)))
            return out

        return online_step(carry, s_raw, pv, kpos, mb)

    init = (jnp.full((R, 1), NEG_INF, F32), jnp.zeros((R, 1), F32), jnp.zeros((R, HEAD_DIM), F32))
    carry = lax.fori_loop(0, n_pages // ppt, past_tile, init)
    n_tail = kt_ref.shape[2]
    ktpos = start + lax.broadcasted_iota(jnp.int32, (1, n_tail), 1)
    mb = jnp.where(ktpos <= qpos, 0.0, NEG_INF)
    _, l_s, acc_s = online_step(carry, _qk(q, kt_ref[0, 0]), lambda p: _pv(p, vt_ref[0, 0]), ktpos, mb)
    o_s = acc_s * (1.0 / l_s)

    kwpos = win_start + lax.broadcasted_iota(jnp.int32, (1, win_keys), 1)
    d_w = qpos_r - kwpos
    ok_w = jnp.logical_and(d_w >= 0, d_w < win_len)
    s = _qk(q, kw_ref[0, 0]) + slope_col * (kwpos - start).astype(F32)
    s = jnp.where(ok_w, s, NEG_INF)
    e = jnp.exp(s - jnp.max(s, axis=-1, keepdims=True))
    o_w = _pv(e, vw_ref[0, 0]) * (1.0 / jnp.sum(e, axis=-1, keepdims=True))

    graw = gate_ref[0]
    gsel = jnp.where(g == 0, graw[:, 0:3 * H], graw[:, 3 * H:6 * H])
    sig = 1.0 / (1.0 + jnp.exp(-gsel))
    for h in range(H):
        rows = slice(h * TQ, (h + 1) * TQ)
        o_h = (sig[:, 3 * h:3 * h + 1] * o_c[rows] + sig[:, 3 * h + 1:3 * h + 2] * o_s[rows]
               + sig[:, 3 * h + 2:3 * h + 3] * o_w[rows])
        o_ref[0, 0, h] = o_h.astype(o_ref.dtype)


LOG2E = 1.4426950408889634


def _nsa_prompt_kernel(q_ref, kc_ref, vc_ref, ks_ref, vs_ref, kw_ref, vw_ref, gate_ref, map_ref, o_ref,
                       *, n_cmp, nsb, win_keys, win_len):
    g = pl.program_id(1)
    i = pl.program_id(2)
    TQ, TK, H, dh = NSA_TQ, NSA_TK, NSA_HPG, HEAD_DIM
    R = H * TQ
    start = i * TQ
    qa = q_ref[0, 0].reshape(R, LANE)
    qpos = start + lax.broadcasted_iota(jnp.int32, (TQ, 1), 0)
    qpos_r = jnp.concatenate([qpos] * H, axis=0)

    kcpos = lax.broadcasted_iota(jnp.int32, (1, n_cmp), 1) * CMP_STRIDE + (CMP_BLOCK - 1)
    ok_c = kcpos <= qpos_r
    s = jnp.where(ok_c, _qk(qa, kc_ref[0, 0]), NEG_INF)
    e = jnp.exp2(s - jnp.max(s, axis=-1, keepdims=True))
    pn = jnp.where(ok_c, e, 0.0) * (1.0 / jnp.sum(e, axis=-1, keepdims=True))
    o_c = _pv(pn, vc_ref[0, 0])
    p_grp = pn[0:TQ]
    for h in range(1, H):
        p_grp = p_grp + pn[h * TQ:(h + 1) * TQ]
    imp = jnp.dot(p_grp, map_ref[...], preferred_element_type=F32, precision=lax.Precision.HIGHEST)

    blk = lax.broadcasted_iota(jnp.int32, (1, nsb), 1)
    blk_f = blk.astype(F32)
    cur = qpos // SEL_BLOCK
    work = jnp.where(blk < cur, imp, -1.0)
    sel = jnp.where(blk == cur, 1.0, 0.0)
    for _ in range(min(SEL_TOPK, nsb) - 1):
        mx = jnp.max(work, axis=-1, keepdims=True)
        first = jnp.min(jnp.where(work == mx, blk_f, float(nsb)), axis=-1, keepdims=True)
        pick = jnp.logical_and(blk_f == first, mx >= 0.0)
        sel = jnp.where(pick, 1.0, sel)
        work = jnp.where(pick, -1.0, work)
    neg_sel = ((sel - 1.0) * (-NEG_INF)).astype(qa.dtype)
    q_full = jnp.concatenate([qa, jnp.concatenate([neg_sel] * H, axis=0)], axis=1)

    def sel_tile(j, carry, diagonal):
        m, acc = carry
        off = pl.multiple_of(j * TK, TK)
        s = _qk(q_full, ks_ref[0, 0, pl.ds(off, TK), :])
        if diagonal:
            kpos = off + lax.broadcasted_iota(jnp.int32, (1, TK), 1)
            s = jnp.where(kpos <= qpos_r, s, NEG_INF)
        m_new = jnp.maximum(m, jnp.max(s, axis=-1, keepdims=True))
        p = jnp.exp2(s - m_new)
        return m_new, jnp.exp2(m - m_new) * acc + _pv(p, vs_ref[0, 0, pl.ds(off, TK), :])

    init = (jnp.full((R, 1), NEG_INF, F32), jnp.zeros((R, LANE), F32))
    j_diag = start // TK
    carry = lax.fori_loop(0, j_diag, lambda j, c: sel_tile(j, c, False), init)
    _, acc_s = sel_tile(j_diag, carry, True)
    o_s = acc_s[:, 0:dh] * (1.0 / acc_s[:, dh:dh + 1])

    ws = pl.multiple_of(jnp.maximum(start - win_len, 0), TQ)
    kwpos = ws + lax.broadcasted_iota(jnp.int32, (1, win_keys), 1)
    d_w = qpos_r - kwpos
    ok_w = jnp.logical_and(d_w >= 0, d_w < win_len)
    s = jnp.where(ok_w, _qk(qa, kw_ref[0, 0, pl.ds(ws, win_keys), :]), NEG_INF)
    e = jnp.exp2(s - jnp.max(s, axis=-1, keepdims=True))
    acc_w = _pv(e, vw_ref[0, 0, pl.ds(ws, win_keys), :])
    o_w = acc_w[:, 0:dh] * (1.0 / acc_w[:, dh:dh + 1])

    graw = gate_ref[0]
    gsel = jnp.where(g == 0, graw[:, 0:3 * H], graw[:, 3 * H:6 * H])
    sig = 1.0 / (1.0 + jnp.exp(-gsel))
    for h in range(H):
        rows = slice(h * TQ, (h + 1) * TQ)
        o_h = (sig[:, 3 * h:3 * h + 1] * o_c[rows] + sig[:, 3 * h + 1:3 * h + 2] * o_s[rows]
               + sig[:, 3 * h + 2:3 * h + 3] * o_w[rows])
        o_ref[0, 0, h] = o_h.astype(o_ref.dtype)


def _nsa_cmp_kernel(x_ref, pe_ref, w1_ref, w2_ref, o_ref):
    x = x_ref[0, 0]
    n, half = x.shape
    a = _dot3(x + pe_ref[0, 0:1, :], w1_ref[0, 0:half, :], ((1,), (0,)))
    b = _dot3(x + pe_ref[0, 1:2, :], w1_ref[0, half:, :], ((1,), (0,)))
    pre = a + pltpu.roll(b, n - 1, 0)
    o_ref[0, 0] = _dot3(_gelu_tanh(pre), w2_ref[0], ((1,), (0,))).astype(o_ref.dtype)


def nsa_compress_blocks(x, pe, w1, w2, out_dtype):
    kv, n, n_chunk, flat = x.shape
    hid = w1.shape[2]
    dh = w2.shape[2]
    return pl.pallas_call(
        _nsa_cmp_kernel,
        grid=(kv, n),
        in_specs=[pl.BlockSpec((1, 1, n_chunk, flat), lambda c, i: (c, i, 0, 0)),
                  pl.BlockSpec((1, 2, flat), lambda c, i: (c, 0, 0)),
                  pl.BlockSpec((1, 2 * flat, hid), lambda c, i: (c, 0, 0)),
                  pl.BlockSpec((1, hid, dh), lambda c, i: (c, 0, 0))],
        out_specs=pl.BlockSpec((1, 1, n_chunk, dh), lambda c, i: (c, i, 0, 0)),
        out_shape=jax.ShapeDtypeStruct((kv, n, n_chunk, dh), out_dtype),
        compiler_params=_cparams("arbitrary", "arbitrary"),
        name="nsa_compress",
    )(x, pe.reshape(kv, 2, flat).astype(F32), w1.astype(x.dtype), w2.astype(x.dtype))


def _nsa_consts(n_cmp, nsb, n_tiles):
    nsb_pad = _round_up(nsb, LANE)
    slopes = jnp.broadcast_to(alibi_slopes().reshape(NSA_KV_HEADS, NSA_HPG, 1, 1), (NSA_KV_HEADS, NSA_HPG, 1, LANE))
    cmap = jnp.pad(cmp_to_sel_map(n_cmp, nsb), ((0, 0), (0, nsb_pad - nsb)))
    key_blk = jnp.arange(n_tiles * NSA_TK, dtype=jnp.int32) // SEL_BLOCK
    expand = (jnp.arange(nsb_pad, dtype=jnp.int32)[:, None] == key_blk[None, :]).astype(BF16)
    expand = expand.reshape(nsb_pad, n_tiles, NSA_TK).transpose(1, 0, 2)
    return slopes, cmap, expand, nsb_pad


def nsa_prompt_attention(proj, B, T, pe, w1, w2, cdt=BF16):
    assert NSA_KV_HEADS == 2 and T % NSA_TK == 0 and NSA_TK % NSA_TQ == 0 and NSA_TQ % SEL_BLOCK == 0
    G, H, dh = NSA_KV_HEADS, NSA_HPG, HEAD_DIM
    nq = T // NSA_TQ
    n_cmp = T // CMP_STRIDE
    nsb = T // SEL_BLOCK
    n_in = proj.shape[1]
    q5 = (proj[:, :NSA_Q_DIM] * (dh ** -0.5 * LOG2E)).astype(cdt).reshape(B, T, G, H, dh).transpose(0, 2, 3, 1, 4)
    kv = proj[:, NSA_Q_DIM:NSA_Q_DIM + NSA_KV_DIM].astype(cdt).reshape(B, T, 6, G, dh)
    xcmp = kv[:, :, 0:2].reshape(B, n_cmp, CMP_STRIDE, 2, G, dh).transpose(3, 0, 4, 1, 2, 5)
    kvc = nsa_compress_blocks(xcmp.reshape(2, B * G, n_cmp, CMP_STRIDE * dh), pe, w1, w2, cdt)
    kvc = kvc.reshape(2, B, G, n_cmp, dh)
    kvt = kv.transpose(2, 0, 3, 1, 4)
    _, cmap, _, nsb_pad = _nsa_consts(n_cmp, nsb, T // NSA_TK)
    assert nsb_pad == LANE

    sl = alibi_slopes() * LOG2E
    s1 = sl.astype(BF16)
    s2 = (sl - s1.astype(F32)).astype(BF16)
    s3 = (sl - s1.astype(F32) - s2.astype(F32)).astype(BF16)
    q_cols = jnp.stack([s1, s2, s3, s1, s2, s3], axis=-1).astype(cdt).reshape(1, G, H, 1, 6)

    def pos_cols(pos):
        hi = (pos // SEL_BLOCK * SEL_BLOCK).astype(cdt)
        lo = (pos % SEL_BLOCK).astype(cdt)
        return jnp.stack([hi, hi, hi, lo, lo, lo], axis=-1)

    def with_cols(x, cols, width):
        n = x.shape[2]
        parts = [x, jnp.broadcast_to(cols, (B, G, n, cols.shape[-1]))]
        parts.append(jnp.zeros((B, G, n, width - dh - cols.shape[-1]), cdt))
        return jnp.concatenate(parts, axis=-1)

    pos = jnp.arange(T, dtype=jnp.int32)
    kcols = pos_cols(pos)
    onehot = (pos[:, None] // SEL_BLOCK == jnp.arange(nsb_pad, dtype=jnp.int32)[None, :]).astype(cdt)
    q_aug = jnp.concatenate([q5, jnp.broadcast_to(q_cols, (B, G, H, T, 6)),
                             jnp.zeros((B, G, H, T, LANE - dh - 6), cdt)], axis=-1)
    ks_aug = jnp.concatenate([with_cols(kvt[2], kcols, LANE),
                              jnp.broadcast_to(onehot, (B, G, T, nsb_pad))], axis=-1)
    kw_aug = with_cols(kvt[4], kcols, LANE)
    kc_aug = with_cols(kvc[0], pos_cols(jnp.arange(n_cmp, dtype=jnp.int32) * CMP_STRIDE + (CMP_BLOCK - 1)), LANE)
    ones_col = jnp.ones((1, 1, 1, 1), cdt)
    vs_aug = with_cols(kvt[3], ones_col, LANE)
    vw_aug = with_cols(kvt[5], ones_col, LANE)

    gate_col = (NSA_Q_DIM + NSA_KV_DIM) // LANE
    seq = lambda w: pl.BlockSpec((1, 1, T, w), lambda b, g, i: (b, g, 0, 0))
    kern = functools.partial(_nsa_prompt_kernel, n_cmp=n_cmp, nsb=nsb_pad, win_keys=WINDOW + NSA_TQ, win_len=WINDOW)
    o5 = pl.pallas_call(
        kern,
        grid=(B, G, nq),
        in_specs=[pl.BlockSpec((1, 1, H, NSA_TQ, LANE), lambda b, g, i: (b, g, 0, i, 0)),
                  pl.BlockSpec((1, 1, n_cmp, LANE), lambda b, g, i: (b, g, 0, 0)),
                  pl.BlockSpec((1, 1, n_cmp, dh), lambda b, g, i: (b, g, 0, 0)),
                  seq(2 * LANE), seq(LANE), seq(LANE), seq(LANE),
                  pl.BlockSpec((1, NSA_TQ, LANE), lambda b, g, i: (b * nq + i, 0, gate_col)),
                  pl.BlockSpec(cmap.shape, lambda b, g, i: (0, 0))],
        out_specs=pl.BlockSpec((1, 1, H, NSA_TQ, dh), lambda b, g, i: (b, g, 0, i, 0)),
        out_shape=jax.ShapeDtypeStruct((B, G, H, T, dh), cdt),
        compiler_params=_cparams("arbitrary", "arbitrary", "arbitrary"),
        name="nsa_attention",
    )(q_aug, kc_aug, kvc[1], ks_aug, vs_aug, kw_aug, vw_aug, proj.reshape(B * nq, NSA_TQ, n_in), cmap)
    return o5.transpose(0, 3, 1, 2, 4).reshape(B * T, G * H * dh)


NSA_TQ_DECODE = 8


def nsa_sample_attention(proj, B, T, cache_kv, cache_win, page_table, pe, w1, w2):
    G, H, dh, TQ = NSA_KV_HEADS, NSA_HPG, HEAD_DIM, NSA_TQ_DECODE
    n_pages, page = page_table.shape[1], cache_kv.shape[1]
    past = n_pages * page
    win_len = cache_win.shape[1]
    last_q = past + T - 1
    n_cmp = past // CMP_STRIDE
    assert past % NSA_TK == 0 and NSA_TK % page == 0 and T <= TQ and past % SEL_BLOCK == 0 and T <= SEL_BLOCK
    assert (last_q - (CMP_BLOCK - 1)) // CMP_STRIDE * CMP_STRIDE + CMP_BLOCK <= past
    assert win_len == WINDOW and past >= win_len
    nsb = past // SEL_BLOCK + 1
    n_in = proj.shape[1]
    cpp = page // CMP_STRIDE
    xcmp = cache_kv[page_table, :, 0:2].reshape(B, n_pages, cpp, CMP_STRIDE, 2, G, dh)
    xcmp = xcmp.transpose(4, 0, 5, 1, 2, 3, 6).reshape(2, B * G, n_cmp, CMP_STRIDE * dh)
    kvc = nsa_compress_blocks(xcmp, pe, w1, w2, F32).reshape(2, B, G, n_cmp, dh)
    ks = cache_kv[page_table, :, 2].transpose(0, 1, 3, 4, 2)
    vs = cache_kv[page_table, :, 3].transpose(0, 1, 3, 4, 2)
    kvn = proj[:, NSA_Q_DIM:NSA_Q_DIM + NSA_KV_DIM].reshape(B, T, 6, G, dh)

    def rows_pad(a, n):
        a = a.transpose(0, 2, 1, 3)
        return jnp.pad(a, ((0, 0), (0, 0), (0, n - a.shape[2]), (0, 0)))

    kt, vt = rows_pad(kvn[:, :, 2], SEL_BLOCK), rows_pad(kvn[:, :, 3], SEL_BLOCK)
    win = jnp.concatenate([cache_win, kvn[:, :, 4:]], axis=1)
    win_keys = _round_up(win_len + T, 8)
    kw, vw = rows_pad(win[:, :, 0], win_keys), rows_pad(win[:, :, 1], win_keys)
    q5 = (proj[:, :NSA_Q_DIM] * (dh ** -0.5)).reshape(B, T, G, H, dh).transpose(0, 2, 3, 1, 4)
    q5 = jnp.pad(q5, ((0, 0), (0, 0), (0, 0), (0, TQ - T), (0, 0)))
    gate_col = NSA_Q_DIM + NSA_KV_DIM
    gates = jnp.pad(proj[:, gate_col:gate_col + LANE].reshape(B, T, LANE), ((0, 0), (0, TQ - T), (0, 0)))
    slopes, cmap, expand, nsb_pad = _nsa_consts(n_cmp, nsb, past // NSA_TK)
    per_bg = lambda n: pl.BlockSpec((1, 1, n, dh), lambda b, g, i: (b, g, 0, 0))
    paged = pl.BlockSpec((1, n_pages, 1, dh, page), lambda b, g, i: (b, 0, g, 0, 0))
    kern = functools.partial(_nsa_decode_kernel, tq=TQ, q_start=past, n_cmp=n_cmp, nsb=nsb_pad,
                             win_keys=win_keys, win_len=win_len, win_start=past - win_len)
    o5 = pl.pallas_call(
        kern,
        grid=(B, G, 1),
        in_specs=[pl.BlockSpec((1, 1, H, TQ, dh), lambda b, g, i: (b, g, 0, 0, 0)),
                  per_bg(n_cmp), per_bg(n_cmp), paged, paged, per_bg(win_keys), per_bg(win_keys),
                  pl.BlockSpec((1, TQ, LANE), lambda b, g, i: (b, 0, 0)),
                  pl.BlockSpec((1, H, 1, LANE), lambda b, g, i: (g, 0, 0, 0)),
                  pl.BlockSpec(cmap.shape, lambda b, g, i: (0, 0)),
                  pl.BlockSpec(expand.shape, lambda b, g, i: (0, 0, 0)),
                  per_bg(SEL_BLOCK), per_bg(SEL_BLOCK)],
        out_specs=pl.BlockSpec((1, 1, H, TQ, dh), lambda b, g, i: (b, g, 0, 0, 0)),
        out_shape=jax.ShapeDtypeStruct((B, G, H, TQ, dh), F32),
        compiler_params=_cparams("arbitrary", "arbitrary", "arbitrary"),
        name="nsa_attention_decode",
    )(q5, kvc[0], kvc[1], ks, vs, kw, vw, gates, slopes, cmap, expand, kt, vt)
    o = o5[:, :, :, :T].transpose(0, 3, 1, 2, 4).reshape(B * T, G * H * dh)
    return o, win[:, win.shape[1] - win_len:]


def causal_dwconv(xp, w):
    return lax.conv_general_dilated(xp, w[:, None, :], window_strides=(1,), padding='VALID',
                                    dimension_numbers=('NWC', 'WIO', 'NWC'),
                                    feature_group_count=xp.shape[-1])


def l2norm(x):
    return x * lax.rsqrt(jnp.sum(x * x, axis=-1, keepdims=True) + 1e-6)


def gated_delta_chunked(q, k, v, g, beta, s0, chunk):
    B, T, H, DK = q.shape
    DV = v.shape[-1]
    n = T // chunk

    def blk(a):
        return jnp.moveaxis(a.reshape(B, n, chunk, H, *a.shape[3:]), 3, 2)

    q, k, v, g, beta = blk(q), blk(k), blk(v), blk(g), blk(beta)
    gc = jnp.cumsum(g, axis=-1)
    causal = jnp.tril(jnp.ones((chunk, chunk), dtype=bool))
    strict = jnp.tril(jnp.ones((chunk, chunk), dtype=bool), -1)
    diff = gc[..., :, None] - gc[..., None, :]
    decay = jnp.where(causal, jnp.exp(jnp.where(causal, diff, 0.0)), 0.0)
    kk = jnp.einsum('bnhik,bnhjk->bnhij', k, k)
    lower = jnp.where(strict, beta[..., :, None] * kk * decay, 0.0)
    a_mat = lower + jnp.eye(chunk, dtype=lower.dtype)
    rhs = jnp.concatenate([v * beta[..., None], k * (beta * jnp.exp(gc))[..., None]], axis=-1)
    sol = lax.linalg.triangular_solve(a_mat, rhs, left_side=True, lower=True, unit_diagonal=True)
    u, w = sol[..., :DV], sol[..., DV:]
    qk = jnp.einsum('bnhik,bnhjk->bnhij', q, k) * decay
    q_dec = q * jnp.exp(gc)[..., None]
    k_dec = k * jnp.exp(gc[..., -1:] - gc)[..., None]
    g_last = jnp.exp(gc[..., -1])

    def step(s, xs):
        u_c, w_c, qk_c, qd_c, kd_c, gl_c = xs
        v_new = u_c - jnp.einsum('bhck,bhkv->bhcv', w_c, s)
        o_c = jnp.einsum('bhck,bhkv->bhcv', qd_c, s) + jnp.einsum('bhij,bhjv->bhiv', qk_c, v_new)
        s = s * gl_c[..., None, None] + jnp.einsum('bhck,bhcv->bhkv', kd_c, v_new)
        return s, o_c

    xs = tuple(jnp.moveaxis(a, 1, 0) for a in (u, w, qk, q_dec, k_dec, g_last))
    s, o = lax.scan(step, s0, xs)
    o = jnp.moveaxis(jnp.moveaxis(o, 0, 1), 2, 3).reshape(B, T, H, DV)
    return o, s


def gdn_core(proj, B, T, conv_buf, s0, conv_w, a_log, dt_bias, norm_w, chunk):
    qkv_raw = proj[..., :3 * GDN_DIM]
    z = proj[..., 3 * GDN_DIM:4 * GDN_DIM].reshape(B, T, GDN_HEADS, GDN_DV)
    a = proj[..., 4 * GDN_DIM:4 * GDN_DIM + GDN_HEADS]
    b = proj[..., 4 * GDN_DIM + GDN_HEADS:4 * GDN_DIM + 2 * GDN_HEADS]
    xpad = jnp.concatenate([conv_buf, qkv_raw], axis=1)
    qkv = jax.nn.silu(causal_dwconv(xpad, conv_w))
    q, k, v = jnp.split(qkv, 3, axis=-1)
    q = l2norm(q.reshape(B, T, GDN_HEADS, GDN_DK)) * (GDN_DK ** -0.5)
    k = l2norm(k.reshape(B, T, GDN_HEADS, GDN_DK))
    v = v.reshape(B, T, GDN_HEADS, GDN_DV)
    beta = jax.nn.sigmoid(b)
    g = -jnp.exp(a_log) * jax.nn.softplus(a + dt_bias)
    o, s = gated_delta_chunked(q, k, v, g, beta, s0, chunk)
    of = o * lax.rsqrt(jnp.mean(o * o, axis=-1, keepdims=True) + RMS_EPS) * norm_w
    o = of * jax.nn.silu(z)
    return o.reshape(B, T, GDN_HEADS * GDN_DV), xpad[:, xpad.shape[1] - (CONV_WIDTH - 1):], s


def complex_linear_combine(e1, e2):
    a1r, a1i, b1r, b1i = e1
    a2r, a2i, b2r, b2i = e2
    return (a2r * a1r - a2i * a1i, a2r * a1i + a2i * a1r,
            a2r * b1r - a2i * b1i + b2r, a2r * b1i + a2i * b1r + b2i)


def s5_core(u, B, T, h0r, h0i, lam_re, lam_im, b_re, b_im, c_re, c_im, d_skip, log_dt, w_glu):
    ug = u.reshape(B, T, S5_GROUPS, S5_GROUP)
    dt = jnp.exp(log_dt)[:, None]
    lr, li = lam_re, lam_im
    mag = jnp.exp(lr * dt)
    ar, ai = mag * jnp.cos(li * dt), mag * jnp.sin(li * dt)
    den = lr * lr + li * li
    fr = ((ar - 1.0) * lr + ai * li) / den
    fi = (ai * lr - (ar - 1.0) * li) / den
    bbar_re = fr[..., None] * b_re - fi[..., None] * b_im
    bbar_im = fr[..., None] * b_im + fi[..., None] * b_re
    bu_re = jnp.einsum('gpc,btgc->btgp', bbar_re, ug)
    bu_im = jnp.einsum('gpc,btgc->btgp', bbar_im, ug)
    bu_re = bu_re.at[:, 0].add(ar * h0r - ai * h0i)
    bu_im = bu_im.at[:, 0].add(ar * h0i + ai * h0r)
    a_re = jnp.broadcast_to(ar, bu_re.shape)
    a_im = jnp.broadcast_to(ai, bu_im.shape)
    _, _, hr, hi = lax.associative_scan(complex_linear_combine, (a_re, a_im, bu_re, bu_im), axis=1)
    y = jnp.einsum('gcp,btgp->btgc', c_re, hr) - jnp.einsum('gcp,btgp->btgc', c_im, hi)
    y = y.reshape(B, T, D_MODEL) + d_skip * u.reshape(B, T, D_MODEL)
    z = jax.nn.gelu(y)
    z = z * jax.nn.sigmoid(z @ w_glu)
    return z, hr[:, -1], hi[:, -1]


GDN_CB = 8


def _gdn_kernel(q_ref, k_ref, v_ref, z_ref, wq_ref, wk_ref, wv_ref, gcol_ref, bcol_ref, grow_ref, nw_ref,
                o_ref, sout_ref, s_ref, tq_ref, tk_ref, tv_ref, *, n_chunks, chunk, mdt):
    C, CB = chunk, n_chunks
    rows = C * CB
    i = pl.program_id(2)

    @pl.when(i == 0)
    def _():
        s_ref[...] = jnp.zeros_like(s_ref)
        tq_ref[...] = jnp.zeros_like(tq_ref)
        tk_ref[...] = jnp.zeros_like(tk_ref)
        tv_ref[...] = jnp.zeros_like(tv_ref)

    def conv_act(x_ref, w_ref, tail_ref):
        x = x_ref[...]
        w = w_ref[...]
        xx = jnp.concatenate([tail_ref[...], x], axis=0)
        y = w[CONV_WIDTH - 1:CONV_WIDTH] * x
        for j in range(CONV_WIDTH - 1):
            off = 8 - (CONV_WIDTH - 1) + j
            y = y + w[j:j + 1] * xx[off:off + rows]
        tail_ref[...] = x[rows - 8:rows]
        return y * (1.0 / (1.0 + jnp.exp(-y)))

    def l2n(x):
        return x * lax.rsqrt(jnp.sum(x * x, axis=-1, keepdims=True) + 1e-6)

    def mm(a, b):
        return _dot3(a.astype(mdt), b.astype(mdt), ((1,), (0,)))

    def mm_nt(a, b):
        return _dot3(a.astype(mdt), b.astype(mdt), ((1,), (1,)))

    def mm3(a, b):
        return _dot3(a, b, ((1,), (0,)))

    q = l2n(conv_act(q_ref, wq_ref, tq_ref)) * (GDN_DK ** -0.5)
    k = l2n(conv_act(k_ref, wk_ref, tk_ref))
    v = conv_act(v_ref, wv_ref, tv_ref)

    ii = lax.broadcasted_iota(jnp.int32, (C, C), 0)
    jj = lax.broadcasted_iota(jnp.int32, (C, C), 1)
    causal = jj <= ii
    strict = jj < ii
    eye = jnp.where(ii == jj, 1.0, 0.0)

    prep = []
    for c in range(CB):
        r = slice(c * C, (c + 1) * C)
        qc, kc, vc = q[r], k[r], v[r]
        gcl = jnp.broadcast_to(gcol_ref[0, 0, r, :], (C, GDN_DV))
        bcl = jnp.broadcast_to(bcol_ref[0, 0, r, :], (C, GDN_DV))
        gr = jnp.broadcast_to(grow_ref[0, 0, c:c + 1, :], (C, C))
        dec = jnp.where(causal, jnp.exp(jnp.where(causal, gcl[:, 0:C] - gr, 0.0)), 0.0)
        n = jnp.where(strict, bcl[:, 0:C] * mm_nt(kc, kc) * dec, 0.0)
        t = eye - n
        p = n
        width = 1
        while 2 * width < C:
            p = mm3(p, p)
            t = t + mm3(t, p)
            width *= 2
        eg = jnp.exp(gcl)
        sol = mm3(t, jnp.concatenate([vc * bcl, kc * (bcl * eg)], axis=1))
        gl = gcl[C - 1:C, :]
        prep.append((sol[:, 0:GDN_DV], sol[:, GDN_DV:], mm_nt(qc, kc) * dec, qc * eg,
                     kc * jnp.exp(gl - gcl), jnp.exp(gl)))

    s = s_ref[...]
    outs = []
    for c in range(CB):
        u, w, qk, qd, kd, egl = prep[c]
        vn = u - mm(w, s)
        outs.append(mm(qd, s) + mm(qk, vn))
        s = s * egl + mm(kd.T, vn)
    s_ref[...] = s
    sout_ref[0, 0] = s

    o = jnp.concatenate(outs, axis=0)
    o = o * lax.rsqrt(jnp.mean(o * o, axis=-1, keepdims=True) + RMS_EPS) * nw_ref[...]
    zz = z_ref[...]
    o_ref[...] = (o * (zz * (1.0 / (1.0 + jnp.exp(-zz))))).astype(o_ref.dtype)


def gdn_prompt(proj, B, T, conv_w, a_log, dt_bias, norm_w, mdt=BF16):
    H, C, CB = GDN_HEADS, GDN_CHUNK, GDN_CB
    assert GDN_DK == LANE and GDN_DV == LANE and T % (C * CB) == 0
    n = T // C
    nblk = n // CB
    rows = C * CB
    a = proj[:, 4 * GDN_DIM:4 * GDN_DIM + H]
    b = proj[:, 4 * GDN_DIM + H:4 * GDN_DIM + 2 * H]
    beta = jax.nn.sigmoid(b)
    g = -jnp.exp(a_log) * jax.nn.softplus(a + dt_bias)
    gc = jnp.cumsum(g.reshape(B, n, C, H), axis=2).transpose(0, 3, 1, 2)
    gcol = gc.reshape(B, H, T, 1)
    bcol = beta.reshape(B, T, H).transpose(0, 2, 1).reshape(B, H, T, 1)
    hb = GDN_DIM // LANE
    col = lambda k: pl.BlockSpec((rows, LANE), lambda b_, h, i: (b_ * nblk + i, k * hb + h))
    wcol = lambda k: pl.BlockSpec((CONV_WIDTH, LANE), lambda b_, h, i: (0, k * hb + h))
    vec = pl.BlockSpec((1, 1, rows, 1), lambda b_, h, i: (b_, h, i, 0))
    kern = functools.partial(_gdn_kernel, n_chunks=CB, chunk=C, mdt=mdt)
    o, s = pl.pallas_call(
        kern,
        grid=(B, H, nblk),
        in_specs=[col(0), col(1), col(2), col(3), wcol(0), wcol(1), wcol(2), vec, vec,
                  pl.BlockSpec((1, 1, CB, C), lambda b_, h, i: (b_, h, i, 0)),
                  pl.BlockSpec((1, LANE), lambda b_, h, i: (0, 0))],
        out_specs=[pl.BlockSpec((rows, LANE), lambda b_, h, i: (b_ * nblk + i, h)),
                   pl.BlockSpec((1, 1, GDN_DK, GDN_DV), lambda b_, h, i: (b_, h, 0, 0))],
        out_shape=[jax.ShapeDtypeStruct((B * T, H * GDN_DV), mdt),
                   jax.ShapeDtypeStruct((B, H, GDN_DK, GDN_DV), F32)],
        scratch_shapes=[pltpu.VMEM((GDN_DK, GDN_DV), F32)] + [pltpu.VMEM((8, LANE), F32)] * 3,
        compiler_params=_cparams("arbitrary", "arbitrary", "arbitrary"),
        name="gdn_prefill",
    )(proj, proj, proj, proj, conv_w, conv_w, conv_w, gcol, bcol, gc, norm_w.reshape(1, LANE))
    cv = proj[:, :3 * GDN_DIM].reshape(B, T, 3 * GDN_DIM)[:, T - (CONV_WIDTH - 1):]
    return o, s, cv


S5_LANES = S5_GROUPS * S5_STATE
S5_SETS = D_MODEL // LANE
S5_SET_LANES = S5_LANES // S5_SETS


def _gelu_tanh(x):
    return x * (0.5 * (1.0 + jnp.tanh(math.sqrt(2.0 / math.pi) * (x + 0.044715 * (x * x * x)))))


def _permute_rows(pm, x, exact):
    x1 = x.astype(BF16)
    out = jnp.dot(pm, x1, preferred_element_type=F32)
    if exact:
        r1 = x - x1.astype(F32)
        x2 = r1.astype(BF16)
        x3 = (r1 - x2.astype(F32)).astype(BF16)
        out = out + (jnp.dot(pm, x2, preferred_element_type=F32) + jnp.dot(pm, x3, preferred_element_type=F32))
    return out


def _s5_kernel(u_ref, x_ref, g_ref, h0r_ref, h0i_ref, ar_ref, ai_ref, bb_ref, cre_ref, cim_ref, d_ref,
               wglu_ref, wout_ref, pm_ref, pmt_ref, o_ref, fr_ref, fi_ref, hr_ref, hi_ref, cr_ref, ci_ref,
               pwr_ref, pwi_ref, *, n_streams, n_steps, chain, chunk_lanes):
    S, L, CW = n_streams, n_steps, chunk_lanes
    i = pl.program_id(1)
    exact = bb_ref.dtype == F32

    @pl.when(i == 0)
    def _():
        pr, pi = ar_ref[...], ai_ref[...]
        a_r, a_i = pr, pi
        pwr_ref[0:1, :] = pr
        pwi_ref[0:1, :] = pi
        for l in range(1, L):
            pr, pi = pr * a_r - pi * a_i, pr * a_i + pi * a_r
            pwr_ref[l:l + 1, :] = pr
            pwi_ref[l:l + 1, :] = pi
        if chain:
            cr_ref[...] = h0r_ref[0]
            ci_ref[...] = h0i_ref[0]

    nb = CW // LANE

    def load_rows(ref, l, ch):
        rows = pl.ds(pl.multiple_of(l * S, S), S)
        parts = [ref[ch * nb + q, rows, :] for q in range(nb)]
        return parts[0] if nb == 1 else jnp.concatenate(parts, axis=1)

    def store_rows(ref, l, ch, val):
        rows = pl.ds(pl.multiple_of(l * S, S), S)
        for q in range(nb):
            ref[ch * nb + q, rows, :] = val[:, q * LANE:(q + 1) * LANE]

    u_p = _permute_rows(pm_ref[...], u_ref[...], exact)
    bps = S5_SET_LANES // LANE
    for s in range(S5_SETS):
        res = _dot(u_p[:, s * LANE:(s + 1) * LANE], bb_ref[s])
        for q in range(bps):
            hr_ref[s * bps + q] = res[:, q * LANE:(q + 1) * LANE]
            hi_ref[s * bps + q] = res[:, S5_SET_LANES + q * LANE:S5_SET_LANES + (q + 1) * LANE]

    for ch in range(S5_LANES // CW):
        lanes = slice(ch * CW, (ch + 1) * CW)
        a_r = jnp.broadcast_to(ar_ref[:, lanes], (S, CW))
        a_i = jnp.broadcast_to(ai_ref[:, lanes], (S, CW))
        if chain:
            init = (jnp.zeros((S, CW), F32), jnp.zeros((S, CW), F32))
        else:
            init = (h0r_ref[0, :, lanes], h0i_ref[0, :, lanes])

        def scan_body(l, carry, ch=ch, a_r=a_r, a_i=a_i):
            h_r, h_i = carry
            n_r = a_r * h_r - a_i * h_i + load_rows(hr_ref, l, ch)
            n_i = a_r * h_i + a_i * h_r + load_rows(hi_ref, l, ch)
            store_rows(hr_ref, l, ch, n_r)
            store_rows(hi_ref, l, ch, n_i)
            return n_r, n_i

        e_r, e_i = lax.fori_loop(0, L, scan_body, init, unroll=min(L, 8))
        if chain:
            al_r, al_i = pwr_ref[L - 1:L, lanes], pwi_ref[L - 1:L, lanes]
            s_r, s_i = cr_ref[:, lanes], ci_ref[:, lanes]
            before_r, before_i = [], []
            for c in range(S):
                before_r.append(s_r)
                before_i.append(s_i)
                s_r, s_i = (e_r[c:c + 1] + al_r * s_r - al_i * s_i, e_i[c:c + 1] + al_r * s_i + al_i * s_r)
            cr_ref[:, lanes] = s_r
            ci_ref[:, lanes] = s_i
            b_r = jnp.concatenate(before_r, axis=0)
            b_i = jnp.concatenate(before_i, axis=0)

            def fix_body(l, carry, ch=ch, lanes=lanes, b_r=b_r, b_i=b_i):
                p_r = pwr_ref[pl.ds(l, 1), lanes]
                p_i = pwi_ref[pl.ds(l, 1), lanes]
                store_rows(hr_ref, l, ch, load_rows(hr_ref, l, ch) + (p_r * b_r - p_i * b_i))
                store_rows(hi_ref, l, ch, load_rows(hi_ref, l, ch) + (p_r * b_i + p_i * b_r))
                return carry

            lax.fori_loop(0, L, fix_body, 0, unroll=min(L, 8))
        else:
            fr_ref[0, :, lanes] = e_r
            fi_ref[0, :, lanes] = e_i

    if chain:
        fr_ref[0] = cr_ref[...]
        fi_ref[0] = ci_ref[...]

    ys = []
    for s in range(S5_SETS):
        h_r = jnp.concatenate([hr_ref[s * bps + q] for q in range(bps)], axis=1)
        h_i = jnp.concatenate([hi_ref[s * bps + q] for q in range(bps)], axis=1)
        ys.append(_dot(h_r, cre_ref[s]) - _dot(h_i, cim_ref[s]))
    y = _permute_rows(pmt_ref[...], jnp.concatenate(ys, axis=1), True) + d_ref[...] * u_ref[...]
    z = _gelu_tanh(y)
    gate = _dot(z, wglu_ref[...])
    z = z * (1.0 / (1.0 + jnp.exp(-gate)))
    o_ref[...] = x_ref[...] + g_ref[0] * _dot(z, wout_ref[...])


def s5_discretize(lam_re, lam_im, b_re, b_im, c_re, c_im, log_dt, wdtype):
    dt = jnp.exp(log_dt)[:, None]
    mag = jnp.exp(lam_re * dt)
    ar, ai = mag * jnp.cos(lam_im * dt), mag * jnp.sin(lam_im * dt)
    den = lam_re * lam_re + lam_im * lam_im
    fr = ((ar - 1.0) * lam_re + ai * lam_im) / den
    fi = (ai * lam_re - (ar - 1.0) * lam_im) / den
    bbar_re = fr[..., None] * b_re - fi[..., None] * b_im
    bbar_im = fr[..., None] * b_im + fi[..., None] * b_re
    gps = S5_GROUPS // S5_SETS
    eye = jnp.eye(gps, dtype=F32)

    def in_blocks(bbar):
        bb = bbar.reshape(S5_SETS, gps, S5_STATE, S5_GROUP)
        return jnp.einsum('ab,sapc->sacbp', eye, bb).reshape(S5_SETS, LANE, S5_SET_LANES)

    def out_blocks(c):
        cc = c.reshape(S5_SETS, gps, S5_GROUP, S5_STATE)
        return jnp.einsum('ab,sacp->sapbc', eye, cc).reshape(S5_SETS, S5_SET_LANES, LANE)

    bb = jnp.concatenate([in_blocks(bbar_re), in_blocks(bbar_im)], axis=-1).astype(wdtype)
    return (ar.reshape(1, S5_LANES), ai.reshape(1, S5_LANES), bb,
            out_blocks(c_re).astype(wdtype), out_blocks(c_im).astype(wdtype))


def s5_mixer(u, x, g, h0r, h0i, disc, d_skip, w_glu, w_out, *, n_streams, n_steps, chain, chunk_lanes,
             rows_per_batch):
    m, d = u.shape
    tm = n_streams * n_steps
    ar, ai, bb, cre, cim = disc
    nb = h0r.shape[0]
    tiles_per_batch = m // tm // nb
    s0 = h0r.shape[1]
    const2 = lambda b, i: (0, 0)
    const3 = lambda b, i: (0, 0, 0)
    row = lambda b, i: (b * tiles_per_batch + i, 0)
    old_row = jnp.arange(tm, dtype=jnp.int32)
    new_row = (old_row % n_steps) * n_streams + old_row // n_steps
    pm = (jnp.arange(tm, dtype=jnp.int32)[:, None] == new_row[None, :]).astype(BF16)
    kern = functools.partial(_s5_kernel, n_streams=n_streams, n_steps=n_steps, chain=chain,
                             chunk_lanes=chunk_lanes)
    return pl.pallas_call(
        kern,
        grid=(nb, tiles_per_batch),
        in_specs=[pl.BlockSpec((tm, d), row),
                  pl.BlockSpec((tm, d), row),
                  pl.BlockSpec((1, 1, d), lambda b, i: (b, 0, 0)) if g.shape[1] == 1 else
                  pl.BlockSpec((1, tm, d), lambda b, i: (0, b * tiles_per_batch + i, 0)),
                  pl.BlockSpec((1, s0, S5_LANES), lambda b, i: (b, 0, 0)),
                  pl.BlockSpec((1, s0, S5_LANES), lambda b, i: (b, 0, 0)),
                  pl.BlockSpec((1, S5_LANES), const2),
                  pl.BlockSpec((1, S5_LANES), const2),
                  pl.BlockSpec(bb.shape, const3),
                  pl.BlockSpec(cre.shape, const3),
                  pl.BlockSpec(cim.shape, const3),
                  pl.BlockSpec((1, d), const2),
                  pl.BlockSpec((d, d), const2),
                  pl.BlockSpec((d, d), const2),
                  pl.BlockSpec((tm, tm), const2),
                  pl.BlockSpec((tm, tm), const2)],
        out_specs=[pl.BlockSpec((tm, d), row),
                   pl.BlockSpec((1, s0, S5_LANES), lambda b, i: (b, 0, 0)),
                   pl.BlockSpec((1, s0, S5_LANES), lambda b, i: (b, 0, 0))],
        out_shape=[jax.ShapeDtypeStruct((m, d), F32),
                   jax.ShapeDtypeStruct(h0r.shape, F32),
                   jax.ShapeDtypeStruct(h0r.shape, F32)],
        scratch_shapes=[pltpu.VMEM((S5_LANES // LANE, tm, LANE), F32),
                        pltpu.VMEM((S5_LANES // LANE, tm, LANE), F32),
                        pltpu.VMEM((1, S5_LANES), F32), pltpu.VMEM((1, S5_LANES), F32),
                        pltpu.VMEM((n_steps, S5_LANES), F32), pltpu.VMEM((n_steps, S5_LANES), F32)],
        compiler_params=_cparams("arbitrary", "arbitrary"),
        name="s5_mixer",
    )(u, x, g, h0r, h0i, ar, ai, bb, cre, cim, d_skip.reshape(1, d), w_glu, w_out, pm, pm.T)


def _pad_cols(w, n):
    return jnp.pad(w, ((0, 0), (0, n - w.shape[1])))


def kernel(x_prompt, x_sample, cache_nsa_kv, cache_nsa_win, state_gdn_s, state_gdn_conv, state_s5_re, state_s5_im, page_table, c_prompt, c_sample, ada_w, ada_b, norm_mix, norm_ffn, norm_final, nsa_w_in, nsa_cmp_pe, nsa_cmp_w1, nsa_cmp_w2, nsa_w_out, gdn_w_in, gdn_conv_w, gdn_a_log, gdn_dt_bias, gdn_norm, gdn_w_out, s5_w_in, s5_lambda_re, s5_lambda_im, s5_b_re, s5_b_im, s5_c_re, s5_c_im, s5_d, s5_log_dt, s5_w_glu, s5_w_out, ffn_w_gu, ffn_w_down, moe_router, moe_w_gu, moe_w_down):
    Bp, Tp, d = x_prompt.shape
    Bs, Ts, _ = x_sample.shape
    Mp, Ms = Bp * Tp, Bs * Ts
    xp = x_prompt.reshape(Mp, d)
    xs = x_sample.reshape(Ms, d)

    c_all = jnp.concatenate([c_prompt, c_sample], axis=0)
    r_pad = _round_up(c_all.shape[0], 8)
    mods = adaln_all(jnp.pad(c_all, ((0, r_pad - c_all.shape[0]), (0, 0))), ada_w, ada_b)

    def mods_of(i):
        parts = jnp.split(mods[i], 6, axis=-1)
        mp = [p[:Bp].reshape(Bp, 1, d) for p in parts]
        ms = [jnp.repeat(p[Bp:Bp + Bs], Ts, axis=0).reshape(1, Ms, d) for p in parts]
        return mp, ms

    P = dict(rows_per_batch=Tp)
    S = dict(rows_per_batch=Ts)

    nsa_kv_p, nsa_kv_s, nsa_win_p, nsa_win_s = [], [], [], []
    gdn_s_p, gdn_s_s, gdn_conv_p, gdn_conv_s = [], [], [], []
    s5_re_p, s5_re_s, s5_im_p, s5_im_s = [], [], [], []

    HI = "highest"
    for i in range(DEPTH):
        (sh1_p, sc1_p, g1_p, sh2_p, sc2_p, g2_p), (sh1_s, sc1_s, g1_s, sh2_s, sc2_s, g2_s) = mods_of(i)
        j = i // N_MIXERS
        if i % N_MIXERS == 0:
            n_in = _round_up(nsa_w_in.shape[2], LANE)
            w_in = _pad_cols(nsa_w_in[j], n_in)
            w_out = nsa_w_out[j]
            proj_p = ln_matmul(xp, norm_mix[i], sc1_p, sh1_p, w_in.astype(BF16), tm=512, tn=n_in, **P)
            proj_s = ln_matmul(xs, norm_mix[i], sc1_s, sh1_s, w_in, tm=Ms, tn=n_in // 3, **S)
            kv6 = proj_p[:, NSA_Q_DIM:NSA_Q_DIM + NSA_KV_DIM].reshape(Bp, Tp, 6, NSA_KV_HEADS, HEAD_DIM)
            cmp_w = (nsa_cmp_pe[j], nsa_cmp_w1[j], nsa_cmp_w2[j])
            o_p = nsa_prompt_attention(proj_p, Bp, Tp, *cmp_w)
            kv_p, win_p = kv6[:, :, :4], kv6[:, Tp - min(WINDOW, Tp):, 4:]
            o_s, win_s = nsa_sample_attention(proj_s, Bs, Ts, cache_nsa_kv[j], cache_nsa_win[j], page_table, *cmp_w)
            kv_s = proj_s[:, NSA_Q_DIM:NSA_Q_DIM + NSA_KV_DIM].reshape(Bs, Ts, 6, NSA_KV_HEADS, HEAD_DIM)[:, :, :4]
            nsa_kv_p.append(kv_p); nsa_kv_s.append(kv_s)
            nsa_win_p.append(win_p); nsa_win_s.append(win_s)
        elif i % N_MIXERS == 1:
            n_in = _round_up(gdn_w_in.shape[2], LANE)
            w_in = _pad_cols(gdn_w_in[j], n_in)
            w_out = gdn_w_out[j]
            proj_p = ln_matmul(xp, norm_mix[i], sc1_p, sh1_p, w_in.astype(BF16), tm=256, tn=n_in, **P)
            proj_s = ln_matmul(xs, norm_mix[i], sc1_s, sh1_s, w_in, tm=Ms, tn=n_in // 3, **S)
            o_p, st_p, cv_p = gdn_prompt(proj_p, Bp, Tp, gdn_conv_w[j], gdn_a_log[j], gdn_dt_bias[j], gdn_norm[j])
            with jax.default_matmul_precision(HI):
                o_s, cv_s, st_s = gdn_core(proj_s.reshape(Bs, Ts, n_in), Bs, Ts, state_gdn_conv[j],
                                           state_gdn_s[j], gdn_conv_w[j], gdn_a_log[j], gdn_dt_bias[j],
                                           gdn_norm[j], Ts)
            gdn_s_p.append(st_p); gdn_s_s.append(st_s)
            gdn_conv_p.append(cv_p); gdn_conv_s.append(cv_s)
        else:
            w_in = s5_w_in[j]
            w_out = s5_w_out[j]
            u_p = ln_matmul(xp, norm_mix[i], sc1_p, sh1_p, w_in.astype(BF16), tm=512, tn=d, **P)
            u_s = ln_matmul(xs, norm_mix[i], sc1_s, sh1_s, w_in, tm=Ms, tn=d, **S)
            s5p = (s5_lambda_re[j], s5_lambda_im[j], s5_b_re[j], s5_b_im[j], s5_c_re[j], s5_c_im[j], s5_log_dt[j])
            h00 = jnp.zeros((Bp, 1, S5_LANES), F32)
            xp, re_p, im_p = s5_mixer(u_p, xp, g1_p, h00, h00, s5_discretize(*s5p, BF16), s5_d[j],
                                      s5_w_glu[j].astype(BF16), w_out.astype(BF16),
                                      n_streams=8, n_steps=32, chain=True, chunk_lanes=512, **P)
            xs, re_s, im_s = s5_mixer(u_s, xs, g1_s, state_s5_re[j].reshape(1, Bs, S5_LANES),
                                      state_s5_im[j].reshape(1, Bs, S5_LANES), s5_discretize(*s5p, F32),
                                      s5_d[j], s5_w_glu[j], w_out,
                                      n_streams=Bs, n_steps=Ts, chain=False, chunk_lanes=LANE, **S)
            st_shape = (-1, S5_GROUPS, S5_STATE)
            s5_re_p.append(re_p.reshape(st_shape)); s5_re_s.append(re_s.reshape(st_shape))
            s5_im_p.append(im_p.reshape(st_shape)); s5_im_s.append(im_s.reshape(st_shape))
        if i % N_MIXERS != 2:
            xp = matmul_residual(o_p.reshape(Mp, -1), w_out.astype(BF16), xp, g1_p, tm=512, **P)
            xs = matmul_residual(o_s.reshape(Ms, -1), w_out, xs, g1_s, tm=Ms, **S)

        f = i // 2
        if i % 2 == 0:
            w_gu, w_down = ffn_w_gu[f], ffn_w_down[f]
            xp = dense_ffn(xp, norm_ffn[i], sc2_p, sh2_p, g2_p, w_gu.astype(BF16), w_down.astype(BF16),
                           tm=1024, tf=256, **P)
            xs = dense_ffn(xs, norm_ffn[i], sc2_s, sh2_s, g2_s, w_gu, w_down, tm=Ms, tf=256, **S)
        else:
            w_gu, w_down = moe_w_gu[f], moe_w_down[f]
            w_gu_b, w_down_b = w_gu.astype(BF16), w_down.astype(BF16)
            xp = moe_layer(xp, norm_ffn[i], sc2_p, sh2_p, g2_p, moe_router[f], w_gu_b, w_down_b,
                           tm_ln=512, tm=512, tf=512, **P)
            last = i == DEPTH - 1
            xs = moe_layer(xs, norm_ffn[i], sc2_s, sh2_s, g2_s, moe_router[f],
                           w_gu_b if last else w_gu, w_down_b if last else w_down,
                           tm_ln=Ms, tm=128, tf=512, **S)

    def final_norm(x):
        return x * lax.rsqrt(jnp.mean(x * x, axis=-1, keepdims=True) + RMS_EPS) * norm_final

    y_prompt = final_norm(xp).reshape(Bp, Tp, d)
    y_sample = final_norm(xs).reshape(Bs, Ts, d)
    return (y_prompt, y_sample, jnp.stack(nsa_kv_p), jnp.stack(nsa_kv_s), jnp.stack(nsa_win_p),
            jnp.stack(nsa_win_s), jnp.stack(gdn_s_p), jnp.stack(gdn_s_s), jnp.stack(gdn_conv_p),
            jnp.stack(gdn_conv_s), jnp.stack(s5_re_p), jnp.stack(s5_re_s), jnp.stack(s5_im_p),
            jnp.stack(s5_im_s))
```
